```python
import math
import jax, jax.numpy as jnp
from jax import lax
import numpy as np

D_MODEL = 1024
BATCH = 1
SEQ = 16384
DEPTH = 2
DEC_BATCH = 32
DEC_SEQ = 8
PAST_LEN = 16384
PAGE_SIZE = 128

HEAD_DIM = 64
N_HEADS_A = 8
N_KV_HEADS_A = 2
N_IDX_HEADS = 8
IDX_DIM = 64
MAX_SELECT = 256
N_HEADS_B = 8
ROPE_THETA = 500000.0
Q_BLOCK = 128
N_MEM = 256
N_MEM_HEADS = 4
MEM_HEAD_DIM = 128
N_EXPERTS = 16
N_GROUPS = 4
EXPERTS_PER_GROUP = N_EXPERTS // N_GROUPS
TOP_K = 2
D_EXPERT = 512
LN_EPS = 1e-5
DEEPNORM_ALPHA = (2 * DEPTH) ** 0.25
DEEPNORM_BETA = (8 * DEPTH) ** -0.25
IN_SIZES = (N_HEADS_A * HEAD_DIM, N_KV_HEADS_A * HEAD_DIM, N_KV_HEADS_A * HEAD_DIM,
            N_IDX_HEADS * IDX_DIM, IDX_DIM, N_IDX_HEADS,
            N_HEADS_B * HEAD_DIM, N_HEADS_B * HEAD_DIM, N_HEADS_B * HEAD_DIM,
            D_MODEL, D_MODEL)
W_IN = sum(IN_SIZES)

kernel_name = 'dsa_stickbreak_gated_moe_decoder_step'


def layer_norm(x, g, b):
    xf = x.astype(jnp.float32)
    mu = xf.mean(-1, keepdims=True)
    var = jnp.square(xf - mu).mean(-1, keepdims=True)
    return ((xf - mu) * lax.rsqrt(var + LN_EPS) * g.astype(jnp.float32) + b.astype(jnp.float32)).astype(x.dtype)


def rope(x, pos):
    rot = x.shape[-1] // 4
    half = rot // 2
    inv_freq = jnp.power(ROPE_THETA, -jnp.arange(half, dtype=jnp.float32) * 2.0 / rot)
    ang = pos.astype(jnp.float32)[:, None] * inv_freq[None, :]
    cos = jnp.cos(ang)[:, None, :]
    sin = jnp.sin(ang)[:, None, :]
    xr = x[..., :rot].astype(jnp.float32)
    x1, x2 = xr[..., :half], xr[..., half:]
    out = jnp.concatenate([x1 * cos - x2 * sin, x2 * cos + x1 * sin], axis=-1).astype(x.dtype)
    return jnp.concatenate([out, x[..., rot:]], axis=-1)


def mixer_inputs(x, w_in_l, pos):
    b, t, _ = x.shape
    offs = []
    acc = 0
    for sz in IN_SIZES[:-1]:
        acc += sz
        offs.append(acc)
    qa, ka, va, qi, ki, wi, qb, kb, vb, ga, gb = jnp.split(x @ w_in_l, offs, axis=-1)
    qa = rope(qa.reshape(b, t, N_HEADS_A, HEAD_DIM), pos)
    ka = rope(ka.reshape(b, t, N_KV_HEADS_A, HEAD_DIM), pos)
    va = va.reshape(b, t, N_KV_HEADS_A, HEAD_DIM)
    qi = rope(qi.reshape(b, t, N_IDX_HEADS, IDX_DIM), pos)
    ki = rope(ki.reshape(b, t, 1, IDX_DIM), pos)[:, :, 0]
    qb = qb.reshape(b, t, N_HEADS_B, HEAD_DIM)
    kb = kb.reshape(b, t, N_HEADS_B, HEAD_DIM)
    vb = vb.reshape(b, t, N_HEADS_B, HEAD_DIM)
    return qa, ka, va, qi, ki, wi, qb, kb, vb, ga, gb


def take_rows(rows, idx):
    return jax.vmap(lambda r, i: r[i])(rows, idx)


def paged_rows(pool, layer, page_table, new_rows, idx):
    db = page_table.shape[0]
    past = page_table.shape[1] * PAGE_SIZE
    p = jnp.minimum(idx, past - 1)
    phys = page_table[jnp.arange(db)[:, None, None], p // PAGE_SIZE]
    from_past = pool[layer, phys, p % PAGE_SIZE]
    from_new = take_rows(new_rows, jnp.clip(idx - past, 0, new_rows.shape[1] - 1))
    return jnp.where((idx < past)[..., None, None], from_past, from_new)


def gathered_pages(pool, layer, page_table):
    g = pool[layer, page_table]
    return g.reshape((g.shape[0], g.shape[1] * PAGE_SIZE) + g.shape[3:])


def dsa_core(q, q_idx, w_idx, k_idx_all, q_pos, topk, gather_kv):
    b, t = q.shape[:2]
    s = jnp.einsum('bthd,bsd->bths', q_idx, k_idx_all, preferred_element_type=jnp.float32)
    score = jnp.einsum('bths,bth->bts', jax.nn.relu(s), w_idx.astype(jnp.float32))
    key_pos = jnp.arange(k_idx_all.shape[1])
    score = jnp.where(key_pos[None, None, :] <= q_pos[None, :, None], score, -jnp.inf)
    _, idx = lax.top_k(score, topk)
    valid = idx <= q_pos[None, :, None]
    k_sel, v_sel = gather_kv(idx)
    qg = q.reshape(b, t, N_KV_HEADS_A, N_HEADS_A // N_KV_HEADS_A, HEAD_DIM)
    logits = jnp.einsum('btcgd,btncd->btcgn', qg, k_sel, preferred_element_type=jnp.float32) * (HEAD_DIM ** -0.5)
    logits = jnp.where(valid[:, :, None, None, :], logits, -jnp.inf)
    p = jax.nn.softmax(logits, axis=-1)
    o = jnp.einsum('btcgn,btncd->btcgd', p.astype(v_sel.dtype), v_sel)
    return o.reshape(b, t, N_HEADS_A * HEAD_DIM)


def dsa_prompt(qa, ka, va, qi, ki, wi, topk):
    b, s = qa.shape[:2]
    gather = lambda idx: (take_rows(ka, idx), take_rows(va, idx))

    def blk(i):
        st = i * Q_BLOCK
        sl = lambda a: lax.dynamic_slice_in_dim(a, st, Q_BLOCK, axis=1)
        return dsa_core(sl(qa), sl(qi), sl(wi), ki, st + jnp.arange(Q_BLOCK), topk, gather)

    out = lax.map(blk, jnp.arange(s // Q_BLOCK))
    return out.transpose(1, 0, 2, 3).reshape(b, s, -1)


def dsa_sample(qa, ka, va, qi, ki, wi, pool_k, pool_v, pool_ik, layer, page_table, topk):
    t = qa.shape[1]
    past = page_table.shape[1] * PAGE_SIZE
    k_idx_all = jnp.concatenate([gathered_pages(pool_ik, layer, page_table), ki], axis=1)
    gather = lambda idx: (paged_rows(pool_k, layer, page_table, ka, idx),
                          paged_rows(pool_v, layer, page_table, va, idx))
    return dsa_core(qa, qi, wi, k_idx_all, past + jnp.arange(t), topk, gather)


def stick_weights(z, mask):
    log_beta = jax.nn.log_sigmoid(z)
    log_keep = jnp.where(mask, jax.nn.log_sigmoid(-z), 0.0)
    after = lax.cumsum(log_keep, axis=z.ndim - 1, reverse=True) - log_keep
    return jnp.where(mask, jnp.exp(log_beta + after), 0.0)


def sb_prompt(q, k, v):
    b, s = q.shape[:2]
    key_pos = jnp.arange(s)

    def blk(i):
        st = i * Q_BLOCK
        qb = lax.dynamic_slice_in_dim(q, st, Q_BLOCK, axis=1)
        z = jnp.einsum('bthd,bshd->bhts', qb, k, preferred_element_type=jnp.float32) * (HEAD_DIM ** -0.5)
        q_pos = st + jnp.arange(Q_BLOCK)
        a = stick_weights(z, key_pos[None, :] < q_pos[:, None])
        return jnp.einsum('bhts,bshd->bthd', a.astype(v.dtype), v).reshape(b, Q_BLOCK, -1)

    out = lax.map(blk, jnp.arange(s // Q_BLOCK))
    return out.transpose(1, 0, 2, 3).reshape(b, s, -1)


def sb_sample(q, k, v, pool_k, pool_v, layer, page_table):
    db, t = q.shape[:2]
    past = page_table.shape[1] * PAGE_SIZE
    k_past = gathered_pages(pool_k, layer, page_table)
    v_past = gathered_pages(pool_v, layer, page_table)
    z = jnp.concatenate([
        jnp.einsum('bthd,bshd->bhts', q, k_past, preferred_element_type=jnp.float32),
        jnp.einsum('bthd,bshd->bhts', q, k, preferred_element_type=jnp.float32)], axis=-1) * (HEAD_DIM ** -0.5)
    q_pos = past + jnp.arange(t)
    key_pos = jnp.arange(past + t)
    a = stick_weights(z, key_pos[None, :] < q_pos[:, None]).astype(v.dtype)
    o = (jnp.einsum('bhts,bshd->bthd', a[..., :past], v_past)
         + jnp.einsum('bhts,bshd->bthd', a[..., past:], v))
    return o.reshape(db, t, -1)


def gated_merge(ya, yb, ga, gb, w_pa_l, w_pb_l, w_o_l):
    merged = jax.nn.sigmoid(ga) * (ya @ w_pa_l) + jax.nn.sigmoid(gb) * (yb @ w_pb_l)
    return merged @ w_o_l


def mem_attend(x, mk, mv, w_cq_l, w_co_l):
    b, t, _ = x.shape
    q = (x @ w_cq_l).reshape(b, t, N_MEM_HEADS, MEM_HEAD_DIM)
    logits = jnp.einsum('bthd,bmhd->bhtm', q, mk, preferred_element_type=jnp.float32) * (MEM_HEAD_DIM ** -0.5)
    p = jax.nn.softmax(logits, axis=-1)
    o = jnp.einsum('bhtm,bmhd->bthd', p.astype(mv.dtype), mv)
    return o.reshape(b, t, -1) @ w_co_l


def moe(x, w_router, b_router, w_gate_l, w_up_l, w_down_l):
    b, t, d = x.shape
    xf = x.reshape(b * t, d)
    logits = jnp.einsum('nd,de->ne', xf, w_router, preferred_element_type=jnp.float32) + b_router.astype(jnp.float32)
    probs = jax.nn.softmax(logits, axis=-1)
    group_score = lax.top_k(probs.reshape(-1, N_GROUPS, EXPERTS_PER_GROUP), TOP_K)[0].sum(-1)
    group = jnp.argmax(group_score, axis=-1)
    in_group = (jnp.arange(N_EXPERTS) // EXPERTS_PER_GROUP)[None, :] == group[:, None]
    top_p, top_i = lax.top_k(jnp.where(in_group, probs, -1.0), TOP_K)
    gates = top_p / top_p.sum(-1, keepdims=True)
    combine = jnp.sum(jax.nn.one_hot(top_i, N_EXPERTS, dtype=jnp.float32) * gates[..., None], axis=1).astype(x.dtype)
    y = jnp.zeros_like(xf)
    for e in range(N_EXPERTS):
        h = jax.nn.silu(xf @ w_gate_l[e]) * (xf @ w_up_l[e])
        y = y + combine[:, e:e + 1] * (h @ w_down_l[e])
    return y.reshape(b, t, d)


def layer_tail(x, mix, mk, mv, ln1_g, ln1_b, w_cq_l, w_co_l, ln2_g, ln2_b,
               w_router, b_router, w_gate_l, w_up_l, w_down_l, ln3_g, ln3_b):
    x = layer_norm(DEEPNORM_ALPHA * x + mix, ln1_g, ln1_b)
    x = layer_norm(DEEPNORM_ALPHA * x + mem_attend(x, mk, mv, w_cq_l, w_co_l), ln2_g, ln2_b)
    x = layer_norm(DEEPNORM_ALPHA * x + moe(x, w_router, b_router, w_gate_l, w_up_l, w_down_l), ln3_g, ln3_b)
    return x


def setup_inputs(seed: int = 0) -> dict:
    key = jax.random.key(seed)
    ks = jax.random.split(key, 40)
    f32 = jnp.float32
    nrm = lambda k, shape, scale: jax.random.normal(k, shape, f32) * scale
    n_pages = PAST_LEN // PAGE_SIZE
    used = DEC_BATCH * n_pages
    n_pool = used + max(1, used // 4)
    page_table = jax.random.permutation(ks[0], n_pool)[:used].reshape(DEC_BATCH, n_pages).astype(jnp.int32)
    value_cols = (2, 8)
    col_scale = jnp.concatenate([jnp.full((sz,), DEEPNORM_BETA if i in value_cols else 1.0, f32)
                                 for i, sz in enumerate(IN_SIZES)])
    w_a = N_HEADS_A * HEAD_DIM
    w_b = N_HEADS_B * HEAD_DIM
    w_m = N_MEM_HEADS * MEM_HEAD_DIM
    return {
        'x_prompt': nrm(ks[1], (BATCH, SEQ, D_MODEL), 1.0),
        'x_sample': nrm(ks[2], (DEC_BATCH, DEC_SEQ, D_MODEL), 1.0),
        'cache_a_k': nrm(ks[3], (DEPTH, n_pool, PAGE_SIZE, N_KV_HEADS_A, HEAD_DIM), 1.0),
        'cache_a_v': nrm(ks[4], (DEPTH, n_pool, PAGE_SIZE, N_KV_HEADS_A, HEAD_DIM), DEEPNORM_BETA),
        'cache_idx_k': nrm(ks[5], (DEPTH, n_pool, PAGE_SIZE, IDX_DIM), 1.0),
        'cache_b_k': nrm(ks[6], (DEPTH, n_pool, PAGE_SIZE, N_HEADS_B, HEAD_DIM), 1.0),
        'cache_b_v': nrm(ks[7], (DEPTH, n_pool, PAGE_SIZE, N_HEADS_B, HEAD_DIM), DEEPNORM_BETA),
        'cache_mem_k': nrm(ks[8], (DEPTH, DEC_BATCH, N_MEM, N_MEM_HEADS, MEM_HEAD_DIM), 1.0),
        'cache_mem_v': nrm(ks[9], (DEPTH, DEC_BATCH, N_MEM, N_MEM_HEADS, MEM_HEAD_DIM), DEEPNORM_BETA),
        'page_table': page_table,
        'mem_prompt': nrm(ks[10], (BATCH, N_MEM, D_MODEL), 1.0),
        'w_in': nrm(ks[11], (DEPTH, D_MODEL, W_IN), D_MODEL ** -0.5) * col_scale,
        'w_pa': nrm(ks[12], (DEPTH, w_a, D_MODEL), w_a ** -0.5),
        'w_pb': nrm(ks[13], (DEPTH, w_b, D_MODEL), w_b ** -0.5),
        'w_o': nrm(ks[14], (DEPTH, D_MODEL, D_MODEL), D_MODEL ** -0.5 * DEEPNORM_BETA),
        'ln1_g': 1.0 + nrm(ks[15], (DEPTH, D_MODEL), 0.02),
        'ln1_b': nrm(ks[16], (DEPTH, D_MODEL), 0.02),
        'w_cq': nrm(ks[17], (DEPTH, D_MODEL, w_m), D_MODEL ** -0.5),
        'w_ck': nrm(ks[18], (DEPTH, D_MODEL, w_m), D_MODEL ** -0.5),
        'w_cv': nrm(ks[19], (DEPTH, D_MODEL, w_m), D_MODEL ** -0.5 * DEEPNORM_BETA),
        'w_co': nrm(ks[20], (DEPTH, w_m, D_MODEL), w_m ** -0.5 * DEEPNORM_BETA),
        'ln2_g': 1.0 + nrm(ks[21], (DEPTH, D_MODEL), 0.02),
        'ln2_b': nrm(ks[22], (DEPTH, D_MODEL), 0.02),
        'w_router': nrm(ks[23], (D_MODEL, N_EXPERTS), D_MODEL ** -0.5),
        'b_router': nrm(ks[24], (N_EXPERTS,), 0.01),
        'w_gate': nrm(ks[25], (DEPTH, N_EXPERTS, D_MODEL, D_EXPERT), D_MODEL ** -0.5),
        'w_up': nrm(ks[26], (DEPTH, N_EXPERTS, D_MODEL, D_EXPERT), D_MODEL ** -0.5),
        'w_down': nrm(ks[27], (DEPTH, N_EXPERTS, D_EXPERT, D_MODEL), D_EXPERT ** -0.5 * DEEPNORM_BETA),
        'ln3_g': 1.0 + nrm(ks[28], (DEPTH, D_MODEL), 0.02),
        'ln3_b': nrm(ks[29], (DEPTH, D_MODEL), 0.02),
    }


def reference(x_prompt, x_sample, cache_a_k, cache_a_v, cache_idx_k, cache_b_k, cache_b_v,
              cache_mem_k, cache_mem_v, page_table, mem_prompt,
              w_in, w_pa, w_pb, w_o, ln1_g, ln1_b, w_cq, w_ck, w_cv, w_co, ln2_g, ln2_b,
              w_router, b_router, w_gate, w_up, w_down, ln3_g, ln3_b):
    seq = x_prompt.shape[1]
    t_dec = x_sample.shape[1]
    past = page_table.shape[1] * PAGE_SIZE
    k_prompt = min(MAX_SELECT, seq // 4)
    k_sample = min(MAX_SELECT, (past + t_dec) // 4)
    pos_p = jnp.arange(seq)
    pos_s = past + jnp.arange(t_dec)
    b_p = mem_prompt.shape[0]
    xp, xs = x_prompt, x_sample
    p_ak, p_av, p_ik, p_bk, p_bv, p_mk, p_mv = [], [], [], [], [], [], []
    s_ak, s_av, s_ik, s_bk, s_bv = [], [], [], [], []
    for l in range(DEPTH):
        tail_w = (ln1_g[l], ln1_b[l], w_cq[l], w_co[l], ln2_g[l], ln2_b[l],
                  w_router, b_router, w_gate[l], w_up[l], w_down[l], ln3_g[l], ln3_b[l])
        qa, ka, va, qi, ki, wi, qb, kb, vb, ga, gb = mixer_inputs(xp, w_in[l], pos_p)
        ya = dsa_prompt(qa, ka, va, qi, ki, wi, k_prompt)
        yb = sb_prompt(qb, kb, vb)
        mix = gated_merge(ya, yb, ga, gb, w_pa[l], w_pb[l], w_o[l])
        mk = (mem_prompt @ w_ck[l]).reshape(b_p, -1, N_MEM_HEADS, MEM_HEAD_DIM)
        mv = (mem_prompt @ w_cv[l]).reshape(b_p, -1, N_MEM_HEADS, MEM_HEAD_DIM)
        xp = layer_tail(xp, mix, mk, mv, *tail_w)
        p_ak.append(ka); p_av.append(va); p_ik.append(ki); p_bk.append(kb); p_bv.append(vb)
        p_mk.append(mk); p_mv.append(mv)
        qa, ka, va, qi, ki, wi, qb, kb, vb, ga, gb = mixer_inputs(xs, w_in[l], pos_s)
        ya = dsa_sample(qa, ka, va, qi, ki, wi, cache_a_k, cache_a_v, cache_idx_k, l, page_table, k_sample)
        yb = sb_sample(qb, kb, vb, cache_b_k, cache_b_v, l, page_table)
        mix = gated_merge(ya, yb, ga, gb, w_pa[l], w_pb[l], w_o[l])
        xs = layer_tail(xs, mix, cache_mem_k[l], cache_mem_v[l], *tail_w)
        s_ak.append(ka); s_av.append(va); s_ik.append(ki); s_bk.append(kb); s_bv.append(vb)
    return (xp, xs,
            jnp.stack(p_ak), jnp.stack(p_av), jnp.stack(p_ik), jnp.stack(p_bk), jnp.stack(p_bv),
            jnp.stack(p_mk), jnp.stack(p_mv),
            jnp.stack(s_ak), jnp.stack(s_av), jnp.stack(s_ik), jnp.stack(s_bk), jnp.stack(s_bv))
```

```python
import functools

import jax
import jax.numpy as jnp
from jax import lax
from jax.experimental import pallas as pl
from jax.experimental.pallas import tpu as pltpu

F32 = jnp.float32
BF16 = jnp.bfloat16
I32 = jnp.int32

HEAD_DIM = 64
N_HEADS_A = 8
N_KV_A = 2
GROUP_A = N_HEADS_A // N_KV_A
N_IDX_HEADS = 8
IDX_DIM = 64
N_HEADS_B = 8
MAX_SELECT = 256
ROPE_THETA = 500000.0
PAGE = 128
N_MEM_HEADS = 4
MEM_HEAD_DIM = 128
N_EXPERTS = 16
N_GROUPS = 4
EXPERTS_PER_GROUP = 4
LN_EPS = 1e-5

LANES = 128
SUBLANES = 8
VMEM_LIMIT = 56 * 2**20

INT_MIN = -(2**31)
INT_MAX = 2**31 - 1
M_INIT = -1e30
MASK_BIAS = -2e30
SB_DEAD = -104.0

_SRC = dict(qa=(0, 512), ka=(512, 128), va=(640, 128), qi=(768, 512), ki=(1280, 64), wi=(1344, 8),
            qb=(1352, 512), kb=(1864, 512), vb=(2376, 512), ga=(2888, 1024), gb=(3912, 1024))
_ORDER = ("ga", "gb", "qa", "qi", "qb", "kb", "vb", "ka", "va", "ki", "wi")
_DST = {}
_off = 0
for _n in _ORDER:
    _DST[_n] = (_off, _SRC[_n][1])
    _off += _SRC[_n][1]
W_IN_PACKED = 5120


def _params(sem):
    return pltpu.CompilerParams(dimension_semantics=sem, vmem_limit_bytes=VMEM_LIMIT)


def _mm_kernel(x_ref, w_ref, o_ref):
    o_ref[...] = jnp.dot(x_ref[...], w_ref[...], preferred_element_type=F32).astype(o_ref.dtype)


def matmul(x, w, tm, tn, out_dtype=F32):
    m, k = x.shape
    n = w.shape[1]
    return pl.pallas_call(
        _mm_kernel,
        grid=(m // tm, n // tn),
        in_specs=[pl.BlockSpec((tm, k), lambda i, j: (i, 0)), pl.BlockSpec((k, tn), lambda i, j: (0, j))],
        out_specs=pl.BlockSpec((tm, tn), lambda i, j: (i, j)),
        out_shape=jax.ShapeDtypeStruct((m, n), out_dtype),
        compiler_params=_params(("parallel", "arbitrary")),
        name="matmul",
    )(x, w)


def _merge_kernel(ya_ref, yb_ref, ga_ref, gb_ref, wpa_ref, wpb_ref, wo_ref, o_ref):
    a = jnp.dot(ya_ref[...].astype(BF16), wpa_ref[...], preferred_element_type=F32)
    b = jnp.dot(yb_ref[...].astype(BF16), wpb_ref[...], preferred_element_type=F32)
    merged = jax.nn.sigmoid(ga_ref[...]) * a + jax.nn.sigmoid(gb_ref[...]) * b
    o_ref[...] = jnp.dot(merged.astype(BF16), wo_ref[...], preferred_element_type=F32)


def gated_merge(ya, yb, proj, wpa, wpb, wo, tm):
    m = ya.shape[0]
    d = wo.shape[1]
    full = lambda a: pl.BlockSpec(a.shape, lambda i: (0, 0))
    return pl.pallas_call(
        _merge_kernel,
        grid=(m // tm,),
        in_specs=[pl.BlockSpec((tm, ya.shape[1]), lambda i: (i, 0)),
                  pl.BlockSpec((tm, yb.shape[1]), lambda i: (i, 0)),
                  pl.BlockSpec((tm, d), lambda i: (i, _DST["ga"][0] // d)),
                  pl.BlockSpec((tm, d), lambda i: (i, _DST["gb"][0] // d)),
                  full(wpa), full(wpb), full(wo)],
        out_specs=pl.BlockSpec((tm, d), lambda i: (i, 0)),
        out_shape=jax.ShapeDtypeStruct((m, d), F32),
        compiler_params=_params(("parallel",)),
        name="gated_merge",
    )(ya, yb, proj, proj, wpa, wpb, wo)


def _sortable_key(score):
    bits = lax.bitcast_convert_type(score, I32)
    key = jnp.where(bits < 0, bits ^ INT_MAX, bits)
    return jnp.where(key == -1, 0, key)


def _dsa_prompt_kernel(kidx_ref, qi_ref, wi_ref, k_ref, vt_ref, q_ref, o_ref, key_ref, *, tq, topk, idx_bits):
    i = pl.program_id(0)
    nchunk = i + 1
    qpos = i * tq + lax.broadcasted_iota(I32, (1, tq), 1)
    row_iota = lax.broadcasted_iota(I32, (tq, 1), 0)

    qi = qi_ref[0]
    wi = wi_ref[0]

    def idx_body(c, carry):
        r0 = pl.multiple_of(c * tq, tq)
        s = jnp.dot(kidx_ref[pl.ds(r0, tq), :], qi, preferred_element_type=F32)
        sc = jnp.zeros((tq, tq), F32)
        for h in range(N_IDX_HEADS):
            sc = sc + jnp.maximum(s[:, h * tq:(h + 1) * tq], 0.0) * wi[h:h + 1, :]
        key = _sortable_key(sc)
        key_ref[pl.ds(r0, tq), :] = jnp.where(r0 + row_iota <= qpos, key, INT_MIN)
        return carry

    lax.fori_loop(0, nchunk, idx_body, 0)

    def count(pred):
        def body(c, acc):
            r0 = pl.multiple_of(c * tq, tq)
            hit = jnp.where(pred(key_ref[pl.ds(r0, tq), :], r0 + row_iota), 1, 0)
            return acc + hit.reshape(tq // SUBLANES, SUBLANES, tq).sum(axis=0)
        acc = lax.fori_loop(0, nchunk, body, jnp.zeros((SUBLANES, tq), I32))
        return acc.sum(axis=0, keepdims=True)

    kq = jnp.minimum(qpos + 1, topk)
    zero = jnp.zeros((1, tq), I32)
    prefix = jnp.where(count(lambda k, r: k >= zero) >= kq, 0, INT_MIN)

    def bit_body(bi, prefix):
        cand = prefix | jnp.left_shift(jnp.int32(1), 30 - bi)
        return jnp.where(count(lambda k, r: k >= cand) >= kq, cand, prefix)

    thr = lax.fori_loop(0, 31, bit_body, prefix)
    n_gt = count(lambda k, r: k > thr)
    n_ge = count(lambda k, r: k >= thr)
    need = kq - n_gt

    def tie_search():
        def tie_body(bi, cut):
            cand = cut | jnp.left_shift(jnp.int32(1), idx_bits - 1 - bi)
            below = count(lambda k, r: (k == thr) & (r < cand))
            return jnp.where(below < need, cand, cut)
        return lax.fori_loop(0, idx_bits, tie_body, jnp.zeros((1, tq), I32))

    excess = jnp.max(jnp.where(n_ge - n_gt > need, 1, 0)) > 0
    cut = lax.cond(excess, tie_search, lambda: jnp.full((1, tq), INT_MAX, I32))

    def bias_body(c, carry):
        r0 = pl.multiple_of(c * tq, tq)
        k = key_ref[pl.ds(r0, tq), :]
        sel = (k > thr) | ((k == thr) & (r0 + row_iota <= cut))
        key_ref[pl.ds(r0, tq), :] = lax.bitcast_convert_type(jnp.where(sel, 0.0, MASK_BIAS).astype(F32), I32)
        return carry

    lax.fori_loop(0, nchunk, bias_body, 0)

    nq = GROUP_A * tq
    for c in range(N_KV_A):
        qc = q_ref[0, c]

        def att_body(ch, carry, qc=qc):
            m, l, acc = carry
            r0 = pl.multiple_of(ch * tq, tq)
            bias = lax.bitcast_convert_type(key_ref[pl.ds(r0, tq), :], F32)
            lg = jnp.dot(k_ref[pl.ds(r0, tq), :], qc, preferred_element_type=F32)
            lg = lg + jnp.concatenate([bias] * GROUP_A, axis=1)
            m_new = jnp.maximum(m, lg.max(axis=0, keepdims=True))
            alpha = jnp.exp(m - m_new)
            p = jnp.exp(lg - m_new)
            l = alpha * l + p.sum(axis=0, keepdims=True)
            pv = jnp.dot(vt_ref[:, pl.ds(r0, tq)], p.astype(BF16), preferred_element_type=F32)
            return m_new, l, alpha * acc + pv

        init = (jnp.full((1, nq), M_INIT, F32), jnp.zeros((1, nq), F32), jnp.zeros((N_KV_A * HEAD_DIM, nq), F32))
        _, l, acc = lax.fori_loop(0, nchunk, att_body, init)
        o_ref[0, c] = (acc / l)[c * HEAD_DIM:(c + 1) * HEAD_DIM, :]


def dsa_prompt(qa, ka, va, qi, ki, wi, topk, tq=128):
    s = qa.shape[0]
    nt = s // tq
    kidx = ki.astype(BF16)
    qi_t = qi.reshape(nt, tq, N_IDX_HEADS, IDX_DIM).transpose(0, 3, 2, 1).reshape(nt, IDX_DIM, N_IDX_HEADS * tq).astype(BF16)
    wi_t = wi.reshape(nt, tq, N_IDX_HEADS).transpose(0, 2, 1)
    k2 = ka.reshape(s, N_KV_A * HEAD_DIM).astype(BF16)
    v2t = va.reshape(s, N_KV_A * HEAD_DIM).T.astype(BF16)
    q5 = (qa * (HEAD_DIM ** -0.5)).reshape(nt, tq, N_KV_A, GROUP_A, HEAD_DIM).transpose(0, 2, 4, 3, 1)
    q5 = q5.reshape(nt, N_KV_A, HEAD_DIM, GROUP_A * tq)
    eye = jnp.eye(N_KV_A, dtype=F32)
    q_pad = (eye[None, :, :, None, None] * q5[:, :, None, :, :]).reshape(nt, N_KV_A, N_KV_A * HEAD_DIM, GROUP_A * tq)
    q_pad = q_pad.astype(BF16)
    out = pl.pallas_call(
        functools.partial(_dsa_prompt_kernel, tq=tq, topk=topk, idx_bits=max(1, (s - 1).bit_length())),
        grid=(nt,),
        in_specs=[pl.BlockSpec((s, IDX_DIM), lambda i: (0, 0)),
                  pl.BlockSpec((1, IDX_DIM, N_IDX_HEADS * tq), lambda i: (i, 0, 0)),
                  pl.BlockSpec((1, N_IDX_HEADS, tq), lambda i: (i, 0, 0)),
                  pl.BlockSpec((s, N_KV_A * HEAD_DIM), lambda i: (0, 0)),
                  pl.BlockSpec((N_KV_A * HEAD_DIM, s), lambda i: (0, 0)),
                  pl.BlockSpec((1, N_KV_A, N_KV_A * HEAD_DIM, GROUP_A * tq), lambda i: (i, 0, 0, 0))],
        out_specs=pl.BlockSpec((1, N_KV_A, HEAD_DIM, GROUP_A * tq), lambda i: (i, 0, 0, 0)),
        out_shape=jax.ShapeDtypeStruct((nt, N_KV_A, HEAD_DIM, GROUP_A * tq), F32),
        scratch_shapes=[pltpu.VMEM((s, tq), I32)],
        compiler_params=_params(("parallel",)),
        name="dsa_prompt",
    )(kidx, qi_t, wi_t, k2, v2t, q_pad)
    out = out.reshape(nt, N_KV_A, HEAD_DIM, GROUP_A, tq).transpose(0, 4, 1, 3, 2)
    return out.reshape(s, N_HEADS_A * HEAD_DIM)


def _tri(n):
    return jnp.where(lax.broadcasted_iota(I32, (n, n), 0) > lax.broadcasted_iota(I32, (n, n), 1), 1.0, 0.0).astype(BF16)


def _sb_block(z, mask, v, carry, tri):
    soft = jnp.log1p(jnp.exp(-jnp.abs(z)))
    log_beta = jnp.minimum(z, 0.0) - soft
    log_keep = -jnp.maximum(z, 0.0) - soft
    if mask is not None:
        log_keep = jnp.where(mask, log_keep, 0.0)
    hi = log_keep.astype(BF16)
    r1 = log_keep - hi.astype(F32)
    mid = r1.astype(BF16)
    lo = (r1 - mid.astype(F32)).astype(BF16)
    after = (jnp.dot(hi, tri, preferred_element_type=F32) + jnp.dot(mid, tri, preferred_element_type=F32)
             + jnp.dot(lo, tri, preferred_element_type=F32)) + carry
    a = jnp.exp(log_beta + after)
    if mask is not None:
        a = jnp.where(mask, a, 0.0)
    out = jnp.dot(a.astype(BF16), v, preferred_element_type=F32)
    return out, carry + jnp.sum(log_keep, axis=1, keepdims=True)


def _sb_prompt_kernel(q_ref, k_ref, v_ref, o_ref, *, tq):
    i = pl.program_id(1)
    q2 = q_ref[...]
    lane = lax.broadcasted_iota(I32, (1, 2 * HEAD_DIM), 1)
    qpos = i * tq + lax.broadcasted_iota(I32, (tq, 1), 0)
    key_iota = lax.broadcasted_iota(I32, (1, tq), 1)
    tri = _tri(tq)
    accs = []
    for hh in range(2):
        qh = jnp.where(lane // HEAD_DIM == hh, q2, jnp.zeros_like(q2))

        def cond(st):
            return (st[0] >= 0) & (st[3] > 0)

        def body(st, qh=qh):
            c, carry, acc, _ = st
            r0 = pl.multiple_of(c * tq, tq)
            z = lax.dot_general(qh, k_ref[pl.ds(r0, tq), :], (((1,), (1,)), ((), ())), preferred_element_type=F32)
            mask = (r0 + key_iota) < qpos
            out, carry = _sb_block(z, mask, v_ref[pl.ds(r0, tq), :], carry, tri)
            alive = (jnp.max(carry) > SB_DEAD).astype(I32)
            return c - 1, carry, acc + out, alive

        init = (i, jnp.zeros((tq, 1), F32), jnp.zeros((tq, 2 * HEAD_DIM), F32), jnp.int32(1))
        accs.append(lax.while_loop(cond, body, init)[2])
    o_ref[...] = jnp.where(lane < HEAD_DIM, accs[0], accs[1])


def sb_prompt(proj_bf16_q, k, v, tq=128):
    s, w = k.shape
    npair = w // (2 * HEAD_DIM)
    return pl.pallas_call(
        functools.partial(_sb_prompt_kernel, tq=tq),
        grid=(npair, s // tq),
        in_specs=[pl.BlockSpec((tq, 2 * HEAD_DIM), lambda h, i: (i, h)),
                  pl.BlockSpec((s, 2 * HEAD_DIM), lambda h, i: (0, h)),
                  pl.BlockSpec((s, 2 * HEAD_DIM), lambda h, i: (0, h))],
        out_specs=pl.BlockSpec((tq, 2 * HEAD_DIM), lambda h, i: (i, h)),
        out_shape=jax.ShapeDtypeStruct((s, w), F32),
        compiler_params=_params(("parallel", "parallel")),
        name="sb_prompt",
    )(proj_bf16_q, k, v)


def _sb_sample_kernel(pt_ref, q_ref, knew_ref, vnew_ref, *rest, pps, t_dec):
    k_refs = rest[:pps]
    v_refs = rest[pps:2 * pps]
    o_ref, carry_ref, acc_ref, alive_ref = rest[2 * pps:]
    j = pl.program_id(1)
    rows = N_HEADS_B * t_dec
    q = q_ref[0]
    tri = _tri(PAGE)

    def block(k, v, mask):
        z = lax.dot_general(q, k.astype(BF16), (((1,), (1,)), ((), ())), preferred_element_type=F32)
        out, carry = _sb_block(z, mask, v.astype(BF16), carry_ref[...], tri)
        carry_ref[...] = carry
        acc_ref[...] += out

    @pl.when(j == 0)
    def _():
        carry_ref[...] = jnp.zeros_like(carry_ref)
        acc_ref[...] = jnp.zeros_like(acc_ref)
        t_row = lax.broadcasted_iota(I32, (rows, 1), 0) % t_dec
        block(knew_ref[0], vnew_ref[0], lax.broadcasted_iota(I32, (1, PAGE), 1) < t_row)
        alive_ref[0] = (jnp.max(carry_ref[...]) > SB_DEAD).astype(I32)

    @pl.when((j > 0) & (alive_ref[0] > 0))
    def _():
        for r in range(pps):
            block(k_refs[r][0, 0], v_refs[r][0, 0], None)
        alive_ref[0] = (jnp.max(carry_ref[...]) > SB_DEAD).astype(I32)

    @pl.when(j == pl.num_programs(1) - 1)
    def _():
        lane = lax.broadcasted_iota(I32, (1, N_HEADS_B * HEAD_DIM), 1)
        acc = acc_ref[...]
        out = jnp.zeros((t_dec, N_HEADS_B * HEAD_DIM), F32)
        for h in range(N_HEADS_B):
            out = out + jnp.where(lane // HEAD_DIM == h, acc[h * t_dec:(h + 1) * t_dec, :], 0.0)
        o_ref[0] = out


def _block_diag_rows(q, n_heads):
    b, t, w = q.shape
    d = w // n_heads
    sel = (jnp.arange(w)[None, :] // d == jnp.arange(n_heads)[:, None]).astype(q.dtype)
    return (q[:, None, :, :] * sel[None, :, None, :]).reshape(b, n_heads * t, w)


def _pad_rows(a, n):
    return jnp.pad(a, ((0, 0), (0, n - a.shape[1]), (0, 0)))


def sb_sample(qb, kb, vb, pool_k, pool_v, layer, page_table, pps=8):
    db, t, w = qb.shape
    n_pages = page_table.shape[1]
    pk = pool_k.reshape(pool_k.shape[0], pool_k.shape[1], PAGE, w)
    pv = pool_v.reshape(pool_v.shape[0], pool_v.shape[1], PAGE, w)
    q_bd = _block_diag_rows(qb * (HEAD_DIM ** -0.5), N_HEADS_B).astype(BF16)
    nsteps = n_pages // pps

    def page_spec(r):
        return pl.BlockSpec(
            (1, 1, PAGE, w),
            lambda b, j, pt: (layer, pt[b, n_pages - 1 - (jnp.maximum(j - 1, 0) * pps + r)], 0, 0))

    rows = N_HEADS_B * t
    grid_spec = pltpu.PrefetchScalarGridSpec(
        num_scalar_prefetch=1,
        grid=(db, nsteps + 1),
        in_specs=[pl.BlockSpec((1, rows, w), lambda b, j, pt: (b, 0, 0)),
                  pl.BlockSpec((1, PAGE, w), lambda b, j, pt: (b, 0, 0)),
                  pl.BlockSpec((1, PAGE, w), lambda b, j, pt: (b, 0, 0))]
                 + [page_spec(r) for r in range(pps)] + [page_spec(r) for r in range(pps)],
        out_specs=pl.BlockSpec((1, t, w), lambda b, j, pt: (b, 0, 0)),
        scratch_shapes=[pltpu.VMEM((rows, 1), F32), pltpu.VMEM((rows, w), F32), pltpu.SMEM((1,), I32)],
    )
    return pl.pallas_call(
        functools.partial(_sb_sample_kernel, pps=pps, t_dec=t),
        grid_spec=grid_spec,
        out_shape=jax.ShapeDtypeStruct((db, t, w), F32),
        compiler_params=_params(("parallel", "arbitrary")),
        name="sb_sample",
    )(page_table, q_bd, _pad_rows(kb, PAGE), _pad_rows(vb, PAGE), *([pk] * pps), *([pv] * pps))


def _dsa_sample_select_kernel(pt_ref, qi_ref, wi_ref, inew_ref, *rest, pps, t_dec, topk, past):
    page_refs = rest[:pps]
    bias_ref, key_ref = rest[pps:]
    j = pl.program_id(1)
    qi = qi_ref[0]
    wi = wi_ref[0]
    nkeys = past + PAGE
    nblk = nkeys // LANES

    def page_scores(kpage):
        s = lax.dot_general(qi, kpage.astype(BF16), (((1,), (1,)), ((), ())), preferred_element_type=F32)
        s = jnp.maximum(s, 0.0) * wi
        sc = jnp.zeros((t_dec, PAGE), F32)
        for h in range(N_IDX_HEADS):
            sc = sc + s[h * t_dec:(h + 1) * t_dec, :]
        return _sortable_key(sc)

    for r in range(pps):
        c0 = pl.multiple_of((j * pps + r) * PAGE, PAGE)
        key_ref[:, pl.ds(c0, PAGE)] = page_scores(page_refs[r][0, 0])

    @pl.when(j == pl.num_programs(1) - 1)
    def _():
        t_row = lax.broadcasted_iota(I32, (t_dec, 1), 0)
        lane = lax.broadcasted_iota(I32, (1, LANES), 1)
        key_ref[:, past:past + PAGE] = jnp.where(lane <= t_row, page_scores(inew_ref[0]), INT_MIN)

        def count(pred):
            def body(c, acc):
                c0 = pl.multiple_of(c * LANES, LANES)
                return acc + jnp.where(pred(key_ref[:, pl.ds(c0, LANES)], c0 + lane), 1, 0)
            acc = lax.fori_loop(0, nblk, body, jnp.zeros((t_dec, LANES), I32))
            return acc.sum(axis=1, keepdims=True)

        kq = jnp.minimum(past + t_row + 1, topk)
        zero = jnp.zeros((t_dec, 1), I32)
        prefix = jnp.where(count(lambda k, p: k >= zero) >= kq, 0, INT_MIN)

        def bit_body(bi, prefix):
            cand = prefix | jnp.left_shift(jnp.int32(1), 30 - bi)
            return jnp.where(count(lambda k, p: k >= cand) >= kq, cand, prefix)

        thr = lax.fori_loop(0, 31, bit_body, prefix)
        n_gt = count(lambda k, p: k > thr)
        n_ge = count(lambda k, p: k >= thr)
        need = kq - n_gt
        idx_bits = max(1, (nkeys - 1).bit_length())

        def tie_search():
            def tie_body(bi, cut):
                cand = cut | jnp.left_shift(jnp.int32(1), idx_bits - 1 - bi)
                below = count(lambda k, p: (k == thr) & (p < cand))
                return jnp.where(below < need, cand, cut)
            return lax.fori_loop(0, idx_bits, tie_body, jnp.zeros((t_dec, 1), I32))

        excess = jnp.max(jnp.where(n_ge - n_gt > need, 1, 0)) > 0
        cut = lax.cond(excess, tie_search, lambda: jnp.full((t_dec, 1), INT_MAX, I32))

        def bias_body(c, carry):
            c0 = pl.multiple_of(c * LANES, LANES)
            k = key_ref[:, pl.ds(c0, LANES)]
            sel = (k > thr) | ((k == thr) & (c0 + lane <= cut))
            bias_ref[0, :, pl.ds(c0, LANES)] = jnp.where(sel, 0.0, MASK_BIAS).astype(F32)
            return carry

        lax.fori_loop(0, nblk, bias_body, 0)


def _dsa_sample_attend_kernel(pt_ref, q_ref, bias_ref, bnew_ref, knew_ref, vnew_ref, *rest, pps, t_dec):
    k_refs = rest[:pps]
    v_refs = rest[pps:2 * pps]
    o_ref, m_ref, l_ref, acc_ref = rest[2 * pps:]
    j = pl.program_id(1)
    rows = GROUP_A * t_dec

    @pl.when(j == 0)
    def _():
        m_ref[...] = jnp.full_like(m_ref, M_INIT)
        l_ref[...] = jnp.zeros_like(l_ref)
        acc_ref[...] = jnp.zeros_like(acc_ref)

    def attend(k, v, bias):
        bias4 = jnp.concatenate([bias] * GROUP_A, axis=0)
        for c in range(N_KV_A):
            lg = lax.dot_general(q_ref[0, c], k, (((1,), (1,)), ((), ())), preferred_element_type=F32) + bias4
            m = m_ref[c]
            m_new = jnp.maximum(m, lg.max(axis=1, keepdims=True))
            alpha = jnp.exp(m - m_new)
            p = jnp.exp(lg - m_new)
            l_ref[c] = alpha * l_ref[c] + p.sum(axis=1, keepdims=True)
            acc_ref[c] = alpha * acc_ref[c] + jnp.dot(p.astype(BF16), v, preferred_element_type=F32)
            m_ref[c] = m_new

    k = jnp.concatenate([k_refs[r][0, 0] for r in range(pps)], axis=0).astype(BF16)
    v = jnp.concatenate([v_refs[r][0, 0] for r in range(pps)], axis=0).astype(BF16)
    attend(k, v, bias_ref[0])

    @pl.when(j == pl.num_programs(1) - 1)
    def _():
        attend(knew_ref[0].astype(BF16), vnew_ref[0].astype(BF16), bnew_ref[0])
        for c in range(N_KV_A):
            o_ref[0, c] = acc_ref[c] / l_ref[c]


def dsa_sample(qa, ka, va, qi, ki, wi, pool_k, pool_v, pool_ik, layer, page_table, topk, pps=8):
    db, t = qa.shape[:2]
    n_pages = page_table.shape[1]
    past = n_pages * PAGE
    nsteps = n_pages // pps
    kvw = N_KV_A * HEAD_DIM
    pk = pool_k.reshape(pool_k.shape[0], pool_k.shape[1], PAGE, kvw)
    pv = pool_v.reshape(pool_v.shape[0], pool_v.shape[1], PAGE, kvw)

    def page_spec(width):
        return lambda r: pl.BlockSpec((1, 1, PAGE, width), lambda b, j, pt: (layer, pt[b, j * pps + r], 0, 0))

    hrows = N_IDX_HEADS * t
    qi_r = qi.transpose(0, 2, 1, 3).reshape(db, hrows, IDX_DIM).astype(BF16)
    wi_r = wi.transpose(0, 2, 1).reshape(db, hrows, 1)
    nkeys = past + PAGE
    bias = pl.pallas_call(
        functools.partial(_dsa_sample_select_kernel, pps=pps, t_dec=t, topk=topk, past=past),
        grid_spec=pltpu.PrefetchScalarGridSpec(
            num_scalar_prefetch=1,
            grid=(db, nsteps),
            in_specs=[pl.BlockSpec((1, hrows, IDX_DIM), lambda b, j, pt: (b, 0, 0)),
                      pl.BlockSpec((1, hrows, 1), lambda b, j, pt: (b, 0, 0)),
                      pl.BlockSpec((1, PAGE, IDX_DIM), lambda b, j, pt: (b, 0, 0))]
                     + [page_spec(IDX_DIM)(r) for r in range(pps)],
            out_specs=pl.BlockSpec((1, t, nkeys), lambda b, j, pt: (b, 0, 0)),
            scratch_shapes=[pltpu.VMEM((t, nkeys), I32)],
        ),
        out_shape=jax.ShapeDtypeStruct((db, t, nkeys), F32),
        compiler_params=_params(("parallel", "arbitrary")),
        name="dsa_sample_select",
    )(page_table, qi_r, wi_r, _pad_rows(ki, PAGE), *([pool_ik] * pps))

    q5 = (qa * (HEAD_DIM ** -0.5)).reshape(db, t, N_KV_A, GROUP_A, HEAD_DIM).transpose(0, 2, 3, 1, 4)
    q5 = q5.reshape(db, N_KV_A, GROUP_A * t, HEAD_DIM)
    eye = jnp.eye(N_KV_A, dtype=F32)
    q_pad = (q5[:, :, :, None, :] * eye[None, :, None, :, None]).reshape(db, N_KV_A, GROUP_A * t, kvw).astype(BF16)
    rows = GROUP_A * t
    out = pl.pallas_call(
        functools.partial(_dsa_sample_attend_kernel, pps=pps, t_dec=t),
        grid_spec=pltpu.PrefetchScalarGridSpec(
            num_scalar_prefetch=1,
            grid=(db, nsteps),
            in_specs=[pl.BlockSpec((1, N_KV_A, rows, kvw), lambda b, j, pt: (b, 0, 0, 0)),
                      pl.BlockSpec((1, t, pps * PAGE), lambda b, j, pt: (b, 0, j)),
                      pl.BlockSpec((1, t, PAGE), lambda b, j, pt: (b, 0, past // PAGE)),
                      pl.BlockSpec((1, PAGE, kvw), lambda b, j, pt: (b, 0, 0)),
                      pl.BlockSpec((1, PAGE, kvw), lambda b, j, pt: (b, 0, 0))]
                     + [page_spec(kvw)(r) for r in range(pps)] + [page_spec(kvw)(r) for r in range(pps)],
            out_specs=pl.BlockSpec((1, N_KV_A, rows, kvw), lambda b, j, pt: (b, 0, 0, 0)),
            scratch_shapes=[pltpu.VMEM((N_KV_A, rows, 1), F32), pltpu.VMEM((N_KV_A, rows, 1), F32),
                            pltpu.VMEM((N_KV_A, rows, kvw), F32)],
        ),
        out_shape=jax.ShapeDtypeStruct((db, N_KV_A, rows, kvw), F32),
        compiler_params=_params(("parallel", "arbitrary")),
        name="dsa_sample_attend",
    )(page_table, q_pad, bias, bias, _pad_rows(ka.reshape(db, t, kvw), PAGE), _pad_rows(va.reshape(db, t, kvw), PAGE),
      *([pk] * pps), *([pv] * pps))
    o = out.reshape(db, N_KV_A, GROUP_A, t, N_KV_A, HEAD_DIM)
    o = jnp.stack([o[:, c, :, :, c, :] for c in range(N_KV_A)], axis=1)
    return o.transpose(0, 3, 1, 2, 4).reshape(db, t, N_HEADS_A * HEAD_DIM)


def _mem_kernel(x_ref, wq_ref, mk_ref, mv_ref, wo_ref, o_ref):
    x = x_ref[...].reshape(x_ref.shape[-2:]).astype(BF16)
    mk = mk_ref[...].reshape(mk_ref.shape[-2:]).astype(BF16)
    mv = mv_ref[...].reshape(mv_ref.shape[-2:]).astype(BF16)
    q = jnp.dot(x, wq_ref[...], preferred_element_type=F32).astype(BF16)
    outs = []
    for h in range(N_MEM_HEADS):
        sl = slice(h * MEM_HEAD_DIM, (h + 1) * MEM_HEAD_DIM)
        lg = lax.dot_general(q[:, sl], mk[:, sl], (((1,), (1,)), ((), ())), preferred_element_type=F32)
        lg = lg * (MEM_HEAD_DIM ** -0.5)
        p = jnp.exp(lg - lg.max(axis=1, keepdims=True))
        p = p / p.sum(axis=1, keepdims=True)
        outs.append(jnp.dot(p.astype(BF16), mv[:, sl], preferred_element_type=F32))
    o = jnp.concatenate(outs, axis=1).astype(BF16)
    o_ref[...] = jnp.dot(o, wo_ref[...], preferred_element_type=F32).reshape(o_ref.shape)


def mem_attend(x, mk, mv, wq, wo, tm):
    b, t, d = x.shape
    n_mem, w = mk.shape[1:]
    return pl.pallas_call(
        _mem_kernel,
        grid=(b, t // tm),
        in_specs=[pl.BlockSpec((1, tm, d), lambda bi, i: (bi, i, 0)),
                  pl.BlockSpec(wq.shape, lambda bi, i: (0, 0)),
                  pl.BlockSpec((1, n_mem, w), lambda bi, i: (bi, 0, 0)),
                  pl.BlockSpec((1, n_mem, w), lambda bi, i: (bi, 0, 0)),
                  pl.BlockSpec(wo.shape, lambda bi, i: (0, 0))],
        out_specs=pl.BlockSpec((1, tm, d), lambda bi, i: (bi, i, 0)),
        out_shape=jax.ShapeDtypeStruct((b, t, d), F32),
        compiler_params=_params(("parallel", "parallel")),
        name="mem_attend",
    )(x, wq, mk, mv, wo)


def _split3_nt(w, x):
    nt = lambda a, b: lax.dot_general(a, b, (((1,), (1,)), ((), ())), preferred_element_type=F32)
    wh = w.astype(BF16)
    wl = (w - wh.astype(F32)).astype(BF16)
    xh = x.astype(BF16)
    xl = (x - xh.astype(F32)).astype(BF16)
    return nt(wh, xh) + (nt(wh, xl) + nt(wl, xh))


def _router_kernel(x_ref, w_ref, b_ref, o_ref):
    logits = _split3_nt(w_ref[...], x_ref[...]) + b_ref[...]
    rows = [logits[e:e + 1, :] for e in range(N_EXPERTS)]
    mx = functools.reduce(jnp.maximum, rows)
    ex = [jnp.exp(r - mx) for r in rows]
    tot = functools.reduce(lambda a, b: a + b, ex)
    p = [e / tot for e in ex]
    gscore = []
    for g in range(N_GROUPS):
        a, b, c, d = p[4 * g:4 * g + 4]
        h1, l1, h2, l2 = jnp.maximum(a, b), jnp.minimum(a, b), jnp.maximum(c, d), jnp.minimum(c, d)
        gscore.append(jnp.maximum(h1, h2) + jnp.maximum(jnp.minimum(h1, h2), jnp.maximum(l1, l2)))
    best = gscore[0]
    group = jnp.zeros_like(best, dtype=I32)
    for g in range(1, N_GROUPS):
        better = gscore[g] > best
        best = jnp.where(better, gscore[g], best)
        group = jnp.where(better, g, group)
    sel = []
    for e in range(N_EXPERTS):
        g = e // EXPERTS_PER_GROUP
        rank = jnp.zeros_like(group)
        for o in range(g * EXPERTS_PER_GROUP, (g + 1) * EXPERTS_PER_GROUP):
            if o != e:
                ahead = (p[o] > p[e]) | ((p[o] == p[e]) & (o < e)) if o < e else (p[o] > p[e])
                rank = rank + jnp.where(ahead, 1, 0)
        sel.append((group == g) & (rank < 2))
    top_sum = functools.reduce(lambda a, b: a + b, [jnp.where(s, pe, 0.0) for s, pe in zip(sel, p)])
    o_ref[...] = jnp.concatenate([jnp.where(s, pe / top_sum, 0.0) for s, pe in zip(sel, p)], axis=0)


def router(x, w_router_t, b_router, tm):
    n, d = x.shape
    return pl.pallas_call(
        _router_kernel,
        grid=(n // tm,),
        in_specs=[pl.BlockSpec((tm, d), lambda i: (i, 0)),
                  pl.BlockSpec((N_EXPERTS, d), lambda i: (0, 0)),
                  pl.BlockSpec((N_EXPERTS, 1), lambda i: (0, 0))],
        out_specs=pl.BlockSpec((N_EXPERTS, tm), lambda i: (0, i)),
        out_shape=jax.ShapeDtypeStruct((N_EXPERTS, n), F32),
        compiler_params=_params(("parallel",)),
        name="router",
    )(x, w_router_t, b_router.reshape(N_EXPERTS, 1))


def _moe_kernel(x_ref, wg_ref, wu_ref, wd_ref, comb_ref, o_ref):
    e = pl.program_id(1)
    x = x_ref[...].astype(BF16)
    g = jnp.dot(x, wg_ref[0], preferred_element_type=F32)
    u = jnp.dot(x, wu_ref[0], preferred_element_type=F32)
    h = (g * jax.nn.sigmoid(g)) * u
    y = jnp.dot(h.astype(BF16), wd_ref[0], preferred_element_type=F32)
    lane = lax.broadcasted_iota(I32, (1, N_EXPERTS), 1)
    ce = jnp.sum(jnp.where(lane == e, comb_ref[...], 0.0), axis=1, keepdims=True)

    @pl.when(e == 0)
    def _():
        o_ref[...] = ce * y

    @pl.when(e > 0)
    def _():
        o_ref[...] += ce * y


def moe_experts(x, comb, wg, wu, wd, tm):
    n, d = x.shape
    de = wg.shape[2]
    return pl.pallas_call(
        _moe_kernel,
        grid=(n // tm, N_EXPERTS),
        in_specs=[pl.BlockSpec((tm, d), lambda i, e: (i, 0)),
                  pl.BlockSpec((1, d, de), lambda i, e: (e, 0, 0)),
                  pl.BlockSpec((1, d, de), lambda i, e: (e, 0, 0)),
                  pl.BlockSpec((1, de, d), lambda i, e: (e, 0, 0)),
                  pl.BlockSpec((tm, N_EXPERTS), lambda i, e: (i, 0))],
        out_specs=pl.BlockSpec((tm, d), lambda i, e: (i, 0)),
        out_shape=jax.ShapeDtypeStruct((n, d), F32),
        compiler_params=_params(("parallel", "arbitrary")),
        name="moe_experts",
    )(x, wg, wu, wd, comb)


def _layer_norm(x, g, b):
    mu = x.mean(-1, keepdims=True)
    var = jnp.square(x - mu).mean(-1, keepdims=True)
    return (x - mu) * lax.rsqrt(var + LN_EPS) * g + b


def _rope(x, pos):
    rot = x.shape[-1] // 4
    half = rot // 2
    inv_freq = jnp.power(ROPE_THETA, -jnp.arange(half, dtype=F32) * 2.0 / rot)
    ang = pos.astype(F32)[:, None] * inv_freq[None, :]
    cos = jnp.cos(ang)[:, None, :]
    sin = jnp.sin(ang)[:, None, :]
    x1, x2 = x[..., :half], x[..., half:rot]
    return jnp.concatenate([x1 * cos - x2 * sin, x2 * cos + x1 * sin, x[..., rot:]], axis=-1)


def _pack_w_in(w_in_l):
    cols = [w_in_l[:, _SRC[n][0]:_SRC[n][0] + _SRC[n][1]] for n in _ORDER]
    packed = jnp.concatenate(cols, axis=1)
    return jnp.pad(packed, ((0, 0), (0, W_IN_PACKED - packed.shape[1]))).astype(BF16)


def _take(proj, name):
    o, n = _DST[name]
    return proj[..., o:o + n]


def _mixer_inputs(proj, pos):
    lead = proj.shape[:-1]
    hd = lambda name, h: _take(proj, name).reshape(lead + (h, HEAD_DIM))
    rp = lambda a: _rope(a, pos)
    qa, ka, va = rp(hd("qa", N_HEADS_A)), rp(hd("ka", N_KV_A)), hd("va", N_KV_A)
    qi = rp(hd("qi", N_IDX_HEADS))
    ki = rp(_take(proj, "ki")[..., None, :])[..., 0, :]
    wi = _take(proj, "wi")
    return qa, ka, va, qi, ki, wi, hd("kb", N_HEADS_B), hd("vb", N_HEADS_B)


def _tail(x2, mix, mem_fn, lw, tm):
    alpha = lw["alpha"]
    x2 = _layer_norm(alpha * x2 + mix, lw["ln1_g"], lw["ln1_b"])
    x2 = _layer_norm(alpha * x2 + mem_fn(x2), lw["ln2_g"], lw["ln2_b"])
    comb = router(x2, lw["w_router_t"], lw["b_router"], tm).T
    y = moe_experts(x2, comb, lw["w_gate"], lw["w_up"], lw["w_down"], tm)
    return _layer_norm(alpha * x2 + y, lw["ln3_g"], lw["ln3_b"])


def kernel(x_prompt, x_sample, cache_a_k, cache_a_v, cache_idx_k, cache_b_k, cache_b_v, cache_mem_k, cache_mem_v,
           page_table, mem_prompt, w_in, w_pa, w_pb, w_o, ln1_g, ln1_b, w_cq, w_ck, w_cv, w_co, ln2_g, ln2_b,
           w_router, b_router, w_gate, w_up, w_down, ln3_g, ln3_b):
    depth = w_in.shape[0]
    bp, seq, d = x_prompt.shape
    db, t_dec, _ = x_sample.shape
    n_mem = mem_prompt.shape[1]
    past = page_table.shape[1] * PAGE
    k_prompt = min(MAX_SELECT, seq // 4)
    k_sample = min(MAX_SELECT, (past + t_dec) // 4)
    pos_p = jnp.arange(seq)
    pos_s = past + jnp.arange(t_dec)
    alpha = (2 * depth) ** 0.25
    tm_p = min(1024, seq)
    n_s = db * t_dec
    wm = N_MEM_HEADS * MEM_HEAD_DIM

    xp = x_prompt.reshape(bp * seq, d)
    xs = x_sample.reshape(n_s, d)
    outs = {k: [] for k in ("p_ak", "p_av", "p_ik", "p_bk", "p_bv", "p_mk", "p_mv", "s_ak", "s_av", "s_ik", "s_bk", "s_bv")}
    mem_bf = mem_prompt.reshape(bp * n_mem, d).astype(BF16)
    for l in range(depth):
        lw = dict(alpha=alpha, ln1_g=ln1_g[l], ln1_b=ln1_b[l], ln2_g=ln2_g[l], ln2_b=ln2_b[l],
                  ln3_g=ln3_g[l], ln3_b=ln3_b[l], w_router_t=w_router.T, b_router=b_router,
                  w_gate=w_gate[l].astype(BF16), w_up=w_up[l].astype(BF16), w_down=w_down[l].astype(BF16))
        w_in_p = _pack_w_in(w_in[l])
        wpa, wpb, wo = w_pa[l].astype(BF16), w_pb[l].astype(BF16), w_o[l].astype(BF16)
        wcq, wco = w_cq[l].astype(BF16), w_co[l].astype(BF16)
        wckv = jnp.concatenate([w_ck[l], w_cv[l]], axis=1).astype(BF16)

        proj = matmul(xp.astype(BF16), w_in_p, tm_p, 1024)
        proj_b = proj.reshape(bp, seq, W_IN_PACKED)
        qa, ka, va, qi, ki, wi, kb, vb = _mixer_inputs(proj_b, pos_p)
        ya = jnp.concatenate([dsa_prompt(qa[b], ka[b], va[b], qi[b], ki[b], wi[b], k_prompt) for b in range(bp)], axis=0)
        qb_s = (_take(proj_b, "qb") * (HEAD_DIM ** -0.5)).astype(BF16)
        kb_s = _take(proj_b, "kb").astype(BF16)
        vb_s = _take(proj_b, "vb").astype(BF16)
        yb = jnp.concatenate([sb_prompt(qb_s[b], kb_s[b], vb_s[b]) for b in range(bp)], axis=0)
        mix = gated_merge(ya, yb, proj, wpa, wpb, wo, tm_p)
        mkv = matmul(mem_bf, wckv, bp * n_mem, wm)
        mk, mv = mkv[:, :wm].reshape(bp, n_mem, wm), mkv[:, wm:].reshape(bp, n_mem, wm)
        mem_fn = lambda x2: mem_attend(x2.reshape(bp, seq, d), mk, mv, wcq, wco, tm_p).reshape(bp * seq, d)
        xp = _tail(xp, mix, mem_fn, lw, tm_p)
        outs["p_ak"].append(ka); outs["p_av"].append(va); outs["p_ik"].append(ki)
        outs["p_bk"].append(kb); outs["p_bv"].append(vb)
        outs["p_mk"].append(mk.reshape(bp, n_mem, N_MEM_HEADS, MEM_HEAD_DIM))
        outs["p_mv"].append(mv.reshape(bp, n_mem, N_MEM_HEADS, MEM_HEAD_DIM))

        proj = matmul(xs.astype(BF16), w_in_p, n_s, 1024)
        proj_b = proj.reshape(db, t_dec, W_IN_PACKED)
        qa, ka, va, qi, ki, wi, kb, vb = _mixer_inputs(proj_b, pos_s)
        ya = dsa_sample(qa, ka, va, qi, ki, wi, cache_a_k, cache_a_v, cache_idx_k, l, page_table, k_sample)
        yb = sb_sample(_take(proj_b, "qb"), _take(proj_b, "kb"), _take(proj_b, "vb"), cache_b_k, cache_b_v, l, page_table)
        mix = gated_merge(ya.reshape(n_s, -1), yb.reshape(n_s, -1), proj, wpa, wpb, wo, n_s)
        cmk = cache_mem_k[l].reshape(db, n_mem, wm)
        cmv = cache_mem_v[l].reshape(db, n_mem, wm)
        mem_fn = lambda x2: mem_attend(x2.reshape(db, t_dec, d), cmk, cmv, wcq, wco, t_dec).reshape(n_s, d)
        xs = _tail(xs, mix, mem_fn, lw, n_s)
        outs["s_ak"].append(ka); outs["s_av"].append(va); outs["s_ik"].append(ki)
        outs["s_bk"].append(kb); outs["s_bv"].append(vb)

    st = lambda k: jnp.stack(outs[k])
    return (xp.reshape(bp, seq, d), xs.reshape(db, t_dec, d),
            st("p_ak"), st("p_av"), st("p_ik"), st("p_bk"), st("p_bv"), st("p_mk"), st("p_mv"),
            st("s_ak"), st("s_av"), st("s_ik"), st("s_bk"), st("s_bv"))
```

```python
import functools

import jax
import jax.numpy as jnp
from jax import lax
from jax.experimental import pallas as pl
from jax.experimental.pallas import tpu as pltpu

F32 = jnp.float32
BF16 = jnp.bfloat16
I32 = jnp.int32

HEAD_DIM = 64
N_HEADS_A = 8
N_KV_A = 2
GROUP_A = N_HEADS_A // N_KV_A
N_IDX_HEADS = 8
IDX_DIM = 64
N_HEADS_B = 8
MAX_SELECT = 256
ROPE_THETA = 500000.0
PAGE = 128
N_MEM_HEADS = 4
MEM_HEAD_DIM = 128
N_EXPERTS = 16
N_GROUPS = 4
EXPERTS_PER_GROUP = 4
LN_EPS = 1e-5

LANES = 128
SUBLANES = 8
VMEM_LIMIT = 56 * 2**20
N_ACC = 4

INT_MIN = -(2**31)
INT_MAX = 2**31 - 1
M_INIT = -1e30
MASK_BIAS = -2e30
LOG2E = 1.4426950408889634
SB_DEAD = -104.0

_SRC = dict(qa=(0, 512), ka=(512, 128), va=(640, 128), qi=(768, 512), ki=(1280, 64), wi=(1344, 8),
            qb=(1352, 512), kb=(1864, 512), vb=(2376, 512), ga=(2888, 1024), gb=(3912, 1024))
_ORDER = ("ga", "gb", "qa", "qi", "qb", "kb", "vb", "ka", "va", "ki", "wi")
_DST = {}
_off = 0
for _n in _ORDER:
    _DST[_n] = (_off, _SRC[_n][1])
    _off += _SRC[_n][1]
W_IN_PACKED = 5120


def _params(sem):
    return pltpu.CompilerParams(dimension_semantics=sem, vmem_limit_bytes=VMEM_LIMIT)


def _dot_nt(a, b):
    return lax.dot_general(a, b, (((1,), (1,)), ((), ())), preferred_element_type=F32)


def _mm_kernel(x_ref, w_ref, o_ref):
    o_ref[...] = jnp.dot(x_ref[...], w_ref[...], preferred_element_type=F32).astype(o_ref.dtype)


def matmul(x, w, tm, tn, out_dtype=F32):
    m, k = x.shape
    n = w.shape[1]
    return pl.pallas_call(
        _mm_kernel,
        grid=(m // tm, n // tn),
        in_specs=[pl.BlockSpec((tm, k), lambda i, j: (i, 0)), pl.BlockSpec((k, tn), lambda i, j: (0, j))],
        out_specs=pl.BlockSpec((tm, tn), lambda i, j: (i, j)),
        out_shape=jax.ShapeDtypeStruct((m, n), out_dtype),
        compiler_params=_params(("parallel", "arbitrary")),
        name="matmul",
    )(x, w)


def _merge_kernel(ya_ref, yb_ref, ga_ref, gb_ref, wpa_ref, wpb_ref, wo_ref, o_ref):
    a = jnp.dot(ya_ref[...].astype(BF16), wpa_ref[...], preferred_element_type=F32)
    b = jnp.dot(yb_ref[...].astype(BF16), wpb_ref[...], preferred_element_type=F32)
    merged = jax.nn.sigmoid(ga_ref[...]) * a + jax.nn.sigmoid(gb_ref[...]) * b
    o_ref[...] = jnp.dot(merged.astype(BF16), wo_ref[...], preferred_element_type=F32)


def gated_merge(ya, yb, proj, wpa, wpb, wo, tm):
    m = ya.shape[0]
    d = wo.shape[1]
    full = lambda a: pl.BlockSpec(a.shape, lambda i: (0, 0))
    return pl.pallas_call(
        _merge_kernel,
        grid=(m // tm,),
        in_specs=[pl.BlockSpec((tm, ya.shape[1]), lambda i: (i, 0)),
                  pl.BlockSpec((tm, yb.shape[1]), lambda i: (i, 0)),
                  pl.BlockSpec((tm, d), lambda i: (i, _DST["ga"][0] // d)),
                  pl.BlockSpec((tm, d), lambda i: (i, _DST["gb"][0] // d)),
                  full(wpa), full(wpb), full(wo)],
        out_specs=pl.BlockSpec((tm, d), lambda i: (i, 0)),
        out_shape=jax.ShapeDtypeStruct((m, d), F32),
        compiler_params=_params(("parallel",)),
        name="gated_merge",
    )(ya, yb, proj, proj, wpa, wpb, wo)


def _sortable_key(score):
    bits = lax.bitcast_convert_type(score, I32)
    key = jnp.where(bits < 0, bits ^ INT_MAX, bits)
    return jnp.where(key == -1, 0, key)


def _select_threshold(count, kq, shape, idx_bits):
    zero = jnp.zeros(shape, I32)
    prefix = jnp.where(count(lambda k, p: k >= zero) >= kq, 0, INT_MIN)

    def bit_body(bi, prefix):
        cand = prefix | jnp.left_shift(jnp.int32(1), 30 - bi)
        return jnp.where(count(lambda k, p: k >= cand) >= kq, cand, prefix)

    thr = lax.fori_loop(0, 31, bit_body, prefix)
    n_gt = count(lambda k, p: k > thr)
    n_ge = count(lambda k, p: k >= thr)
    need = kq - n_gt

    def tie_search():
        def tie_body(bi, cut):
            cand = cut | jnp.left_shift(jnp.int32(1), idx_bits - 1 - bi)
            below = count(lambda k, p: (k == thr) & (p < cand))
            return jnp.where(below < need, cand, cut)
        return lax.fori_loop(0, idx_bits, tie_body, zero)

    excess = jnp.max(jnp.where(n_ge - n_gt > need, 1, 0)) > 0
    cut = lax.cond(excess, tie_search, lambda: jnp.full(shape, INT_MAX, I32))
    return thr, cut


def _dsa_prompt_kernel(kidx_ref, qi_ref, wi_ref, k_ref, vt_ref, q_ref, o_ref, key_ref, *, tq, kc, topk, idx_bits):
    i = pl.program_id(0)
    nchunk = ((i + 1) * tq + kc - 1) // kc
    qpos = i * tq + lax.broadcasted_iota(I32, (1, tq), 1)
    row_iota = lax.broadcasted_iota(I32, (kc, 1), 0)

    qi = qi_ref[0]
    wi = wi_ref[0]

    def idx_body(c, carry):
        r0 = pl.multiple_of(c * kc, kc)
        s = jnp.dot(kidx_ref[pl.ds(r0, kc), :], qi, preferred_element_type=F32)
        sc = jnp.zeros((kc, tq), F32)
        for h in range(N_IDX_HEADS):
            sc = sc + jnp.maximum(s[:, h * tq:(h + 1) * tq], 0.0) * wi[h:h + 1, :]
        key_ref[pl.ds(r0, kc), :] = jnp.where(r0 + row_iota <= qpos, _sortable_key(sc), INT_MIN)
        return carry

    lax.fori_loop(0, nchunk, idx_body, 0)

    def count(pred):
        def body(c, acc):
            r0 = pl.multiple_of(c * kc, kc)
            hit = jnp.where(pred(key_ref[pl.ds(r0, kc), :], r0 + row_iota), 1, 0)
            return acc + hit.reshape(kc // (N_ACC * SUBLANES), N_ACC * SUBLANES, tq).sum(axis=0)
        acc = lax.fori_loop(0, nchunk, body, jnp.zeros((N_ACC * SUBLANES, tq), I32))
        return acc.sum(axis=0, keepdims=True)

    thr, cut = _select_threshold(count, jnp.minimum(qpos + 1, topk), (1, tq), idx_bits)

    def bias_body(c, carry):
        r0 = pl.multiple_of(c * kc, kc)
        k = key_ref[pl.ds(r0, kc), :]
        sel = (k > thr) | ((k == thr) & (r0 + row_iota <= cut))
        key_ref[pl.ds(r0, kc), :] = lax.bitcast_convert_type(jnp.where(sel, 0.0, MASK_BIAS).astype(F32), I32)
        return carry

    lax.fori_loop(0, nchunk, bias_body, 0)

    nq = GROUP_A * tq

    def att_body(ch, carry):
        r0 = pl.multiple_of(ch * kc, kc)
        bias = lax.bitcast_convert_type(key_ref[pl.ds(r0, kc), :], F32)
        bias = jnp.concatenate([bias] * GROUP_A, axis=1)
        kblk = k_ref[pl.ds(r0, kc), :]
        new = []
        for c in range(N_KV_A):
            m, l, acc = carry[c]
            lg = jnp.dot(kblk, q_ref[0, c], preferred_element_type=F32) + bias
            m_new = jnp.maximum(m, lg.max(axis=0, keepdims=True))
            alpha = jnp.exp2(m - m_new)
            p = jnp.exp2(lg - m_new)
            l = alpha * l + p.sum(axis=0, keepdims=True)
            vt = vt_ref[c * HEAD_DIM:(c + 1) * HEAD_DIM, pl.ds(r0, kc)]
            pv = jnp.dot(vt, p.astype(BF16), preferred_element_type=F32)
            new.append((m_new, l, alpha * acc + pv))
        return tuple(new)

    init = (jnp.full((1, nq), M_INIT, F32), jnp.zeros((1, nq), F32), jnp.zeros((HEAD_DIM, nq), F32))
    res = lax.fori_loop(0, nchunk, att_body, (init,) * N_KV_A)
    for c in range(N_KV_A):
        o_ref[0, c] = res[c][2] / res[c][1]


def dsa_prompt(qa, ka, va, qi, ki, wi, topk, tq=128, kc=512):
    s = qa.shape[0]
    kc = min(kc, s)
    nt = s // tq
    kidx = ki.astype(BF16)
    qi_t = qi.reshape(nt, tq, N_IDX_HEADS, IDX_DIM).transpose(0, 3, 2, 1).reshape(nt, IDX_DIM, N_IDX_HEADS * tq).astype(BF16)
    wi_t = wi.reshape(nt, tq, N_IDX_HEADS).transpose(0, 2, 1)
    k2 = ka.reshape(s, N_KV_A * HEAD_DIM).astype(BF16)
    v2t = va.reshape(s, N_KV_A * HEAD_DIM).T.astype(BF16)
    q5 = (qa * (HEAD_DIM ** -0.5 * LOG2E)).reshape(nt, tq, N_KV_A, GROUP_A, HEAD_DIM).transpose(0, 2, 4, 3, 1)
    q5 = q5.reshape(nt, N_KV_A, HEAD_DIM, GROUP_A * tq)
    eye = jnp.eye(N_KV_A, dtype=F32)
    q_pad = (eye[None, :, :, None, None] * q5[:, :, None, :, :]).reshape(nt, N_KV_A, N_KV_A * HEAD_DIM, GROUP_A * tq)
    q_pad = q_pad.astype(BF16)
    out = pl.pallas_call(
        functools.partial(_dsa_prompt_kernel, tq=tq, kc=kc, topk=topk, idx_bits=max(1, (s - 1).bit_length())),
        grid=(nt,),
        in_specs=[pl.BlockSpec((s, IDX_DIM), lambda i: (0, 0)),
                  pl.BlockSpec((1, IDX_DIM, N_IDX_HEADS * tq), lambda i: (i, 0, 0)),
                  pl.BlockSpec((1, N_IDX_HEADS, tq), lambda i: (i, 0, 0)),
                  pl.BlockSpec((s, N_KV_A * HEAD_DIM), lambda i: (0, 0)),
                  pl.BlockSpec((N_KV_A * HEAD_DIM, s), lambda i: (0, 0)),
                  pl.BlockSpec((1, N_KV_A, N_KV_A * HEAD_DIM, GROUP_A * tq), lambda i: (i, 0, 0, 0))],
        out_specs=pl.BlockSpec((1, N_KV_A, HEAD_DIM, GROUP_A * tq), lambda i: (i, 0, 0, 0)),
        out_shape=jax.ShapeDtypeStruct((nt, N_KV_A, HEAD_DIM, GROUP_A * tq), F32),
        scratch_shapes=[pltpu.VMEM((s, tq), I32)],
        compiler_params=_params(("parallel",)),
        name="dsa_prompt",
    )(kidx, qi_t, wi_t, k2, v2t, q_pad)
    out = out.reshape(nt, N_KV_A, HEAD_DIM, GROUP_A, tq).transpose(0, 4, 1, 3, 2)
    return out.reshape(s, N_HEADS_A * HEAD_DIM)


def _tri(n):
    return jnp.where(lax.broadcasted_iota(I32, (n, n), 0) > lax.broadcasted_iota(I32, (n, n), 1), 1.0, 0.0).astype(BF16)


def _sb_block(z, mask, v, carry, tri, v_keys_on_lanes=False):
    soft = jnp.log1p(jnp.exp(-jnp.abs(z)))
    log_beta = jnp.minimum(z, 0.0) - soft
    log_keep = -jnp.maximum(z, 0.0) - soft
    if mask is not None:
        log_keep = jnp.where(mask, log_keep, 0.0)
    hi = log_keep.astype(BF16)
    r1 = log_keep - hi.astype(F32)
    mid = r1.astype(BF16)
    lo = (r1 - mid.astype(F32)).astype(BF16)
    after = (jnp.dot(hi, tri, preferred_element_type=F32) + jnp.dot(mid, tri, preferred_element_type=F32)
             + jnp.dot(lo, tri, preferred_element_type=F32)) + carry
    a = jnp.exp(log_beta + after)
    if mask is not None:
        a = jnp.where(mask, a, 0.0)
    a = a.astype(BF16)
    out = _dot_nt(a, v) if v_keys_on_lanes else jnp.dot(a, v, preferred_element_type=F32)
    return out, carry + jnp.sum(log_keep, axis=1, keepdims=True)


def _sb_prompt_kernel(q_ref, k_ref, v_ref, o_ref, *, tq):
    i = pl.program_id(1)
    q2 = q_ref[...]
    lane = lax.broadcasted_iota(I32, (1, 2 * HEAD_DIM), 1)
    qpos = i * tq + lax.broadcasted_iota(I32, (tq, 1), 0)
    key_iota = lax.broadcasted_iota(I32, (1, tq), 1)
    tri = _tri(tq)
    qh = [jnp.where(lane // HEAD_DIM == hh, q2, jnp.zeros_like(q2)) for hh in range(2)]

    def cond(st):
        return (st[0] >= 0) & (st[1] > 0)

    def body(st):
        c, _, carries, accs = st
        r0 = pl.multiple_of(c * tq, tq)
        kblk = k_ref[pl.ds(r0, tq), :]
        vblk = v_ref[pl.ds(r0, tq), :]
        mask = (r0 + key_iota) < qpos
        new_c, new_a = [], []
        for hh in range(2):
            out, carry = _sb_block(_dot_nt(qh[hh], kblk), mask, vblk, carries[hh], tri)
            new_c.append(carry)
            new_a.append(accs[hh] + out)
        alive = (jnp.max(jnp.maximum(new_c[0], new_c[1])) > SB_DEAD).astype(I32)
        return c - 1, alive, tuple(new_c), tuple(new_a)

    zc = jnp.zeros((tq, 1), F32)
    za = jnp.zeros((tq, 2 * HEAD_DIM), F32)
    accs = lax.while_loop(cond, body, (i, jnp.int32(1), (zc, zc), (za, za)))[3]
    o_ref[...] = jnp.where(lane < HEAD_DIM, accs[0], accs[1])


def sb_prompt(q, k, v, tq=128):
    s, w = k.shape
    npair = w // (2 * HEAD_DIM)
    return pl.pallas_call(
        functools.partial(_sb_prompt_kernel, tq=tq),
        grid=(npair, s // tq),
        in_specs=[pl.BlockSpec((tq, 2 * HEAD_DIM), lambda h, i: (i, h)),
                  pl.BlockSpec((s, 2 * HEAD_DIM), lambda h, i: (0, h)),
                  pl.BlockSpec((s, 2 * HEAD_DIM), lambda h, i: (0, h))],
        out_specs=pl.BlockSpec((tq, 2 * HEAD_DIM), lambda h, i: (i, h)),
        out_shape=jax.ShapeDtypeStruct((s, w), F32),
        compiler_params=_params(("parallel", "parallel")),
        name="sb_prompt",
    )(q, k, v)


def _keys_on_lanes(pool):
    nd = pool.ndim
    return pool.transpose((0, 1) + tuple(range(3, nd)) + (2,))


def _new_keys_on_lanes(a):
    a = a.transpose(0, 2, 1)
    return jnp.pad(a, ((0, 0), (0, 0), (0, PAGE - a.shape[2])))


def _sb_sample_kernel(pt_ref, flag_ref, q_ref, a_ref, b_ref, *rest, pps, t_dec, first):
    k_refs = rest[:pps]
    v_refs = rest[pps:2 * pps]
    outs = rest[2 * pps:]
    rows = N_HEADS_B * t_dec
    w = N_HEADS_B * HEAD_DIM
    q = q_ref[0]
    tri = _tri(PAGE)
    if first:
        acc_ref, carry_ref, alive_ref = outs
        acc_o, carry_o = acc_ref.at[0], carry_ref.at[0]
    else:
        o_ref, acc_o, carry_o, alive_ref = outs
    j = pl.program_id(1)

    def block(kt, vt, mask):
        z = jnp.dot(q, kt.astype(BF16), preferred_element_type=F32)
        out, carry = _sb_block(z, mask, vt.astype(BF16), carry_o[...], tri, v_keys_on_lanes=True)
        carry_o[...] = carry
        acc_o[...] += out
        alive_ref[0] = (jnp.max(carry) > SB_DEAD).astype(I32)

    @pl.when(j == 0)
    def _():
        if first:
            carry_o[...] = jnp.zeros((rows, 1), F32)
            acc_o[...] = jnp.zeros((rows, w), F32)
            t_row = lax.broadcasted_iota(I32, (rows, 1), 0) % t_dec
            block(a_ref[0], b_ref[0], lax.broadcasted_iota(I32, (1, PAGE), 1) < t_row)
        else:
            acc_o[...] = a_ref[0]
            carry_o[...] = b_ref[0]
            alive_ref[0] = flag_ref[pl.program_id(0)]

    for r in range(pps):
        @pl.when(alive_ref[0] > 0)
        def _(r=r):
            block(k_refs[r][0, 0].reshape(w, PAGE), v_refs[r][0, 0].reshape(w, PAGE), None)

    if not first:
        @pl.when(j == pl.num_programs(1) - 1)
        def _():
            lane = lax.broadcasted_iota(I32, (1, w), 1)
            acc = acc_o[...]
            out = jnp.zeros((t_dec, w), F32)
            for h in range(N_HEADS_B):
                out = out + jnp.where(lane // HEAD_DIM == h, acc[h * t_dec:(h + 1) * t_dec, :], 0.0)
            o_ref[0] = out


def _block_diag_rows(q, n_heads):
    b, t, w = q.shape
    d = w // n_heads
    sel = (jnp.arange(w)[None, :] // d == jnp.arange(n_heads)[:, None]).astype(q.dtype)
    return (q[:, None, :, :] * sel[None, :, None, :]).reshape(b, n_heads * t, w)


def sb_sample(qb, kb, vb, pool_k, pool_v, layer, page_table, head_pages=8, pps=24):
    db, t, w = qb.shape
    n_pages = page_table.shape[1]
    head_pages = min(head_pages, n_pages)
    rest = n_pages - head_pages
    while rest % pps:
        pps -= 1
    pk, pv = _keys_on_lanes(pool_k), _keys_on_lanes(pool_v)
    q_bd = _block_diag_rows(qb * (HEAD_DIM ** -0.5), N_HEADS_B).astype(BF16)
    rows = N_HEADS_B * t
    page_block = (1, 1, N_HEADS_B, HEAD_DIM, PAGE)
    batch3 = lambda shape: pl.BlockSpec((1,) + shape, lambda b, j, pt, fl: (b, 0, 0))

    def head_spec(r):
        return pl.BlockSpec(page_block, lambda b, j, pt, fl: (layer, pt[b, n_pages - 1 - r], 0, 0, 0))

    ones = jnp.ones((db,), I32)
    acc, carry = pl.pallas_call(
        functools.partial(_sb_sample_kernel, pps=head_pages, t_dec=t, first=True),
        grid_spec=pltpu.PrefetchScalarGridSpec(
            num_scalar_prefetch=2,
            grid=(db, 1),
            in_specs=[batch3((rows, w)), batch3((w, PAGE)), batch3((w, PAGE))]
                     + [head_spec(r) for r in range(head_pages)] * 2,
            out_specs=[batch3((rows, w)), batch3((rows, 1))],
            scratch_shapes=[pltpu.SMEM((1,), I32)],
        ),
        out_shape=[jax.ShapeDtypeStruct((db, rows, w), F32), jax.ShapeDtypeStruct((db, rows, 1), F32)],
        compiler_params=_params(("parallel", "arbitrary")),
        name="sb_sample_head",
    )(page_table, ones, q_bd, _new_keys_on_lanes(kb), _new_keys_on_lanes(vb), *([pk] * head_pages), *([pv] * head_pages))

    alive = (jnp.max(carry, axis=(1, 2)) > SB_DEAD).astype(I32)

    def tail_spec(r):
        def index(b, j, pt, fl):
            page = pt[b, n_pages - 1 - head_pages - (j * pps + r)]
            return (layer, jnp.where(fl[b] > 0, page, 0), 0, 0, 0)
        return pl.BlockSpec(page_block, index)

    return pl.pallas_call(
        functools.partial(_sb_sample_kernel, pps=pps, t_dec=t, first=False),
        grid_spec=pltpu.PrefetchScalarGridSpec(
            num_scalar_prefetch=2,
            grid=(db, rest // pps),
            in_specs=[batch3((rows, w)), batch3((rows, w)), batch3((rows, 1))]
                     + [tail_spec(r) for r in range(pps)] * 2,
            out_specs=batch3((t, w)),
            scratch_shapes=[pltpu.VMEM((rows, w), F32), pltpu.VMEM((rows, 1), F32), pltpu.SMEM((1,), I32)],
        ),
        out_shape=jax.ShapeDtypeStruct((db, t, w), F32),
        compiler_params=_params(("parallel", "arbitrary")),
        name="sb_sample_tail",
    )(page_table, alive, q_bd, acc, carry, *([pk] * pps), *([pv] * pps))


def _dsa_sample_select_kernel(pt_ref, qi_ref, wi_ref, inew_ref, *rest, pps, t_dec, topk, past, cb):
    page_refs = rest[:pps]
    bias_ref, key_ref = rest[pps:]
    j = pl.program_id(1)
    qi = qi_ref[0]
    wi = wi_ref[0]
    nkeys = past + cb
    nblk = nkeys // cb

    def scores(kt):
        s = jnp.maximum(jnp.dot(qi, kt.astype(BF16), preferred_element_type=F32), 0.0) * wi
        sc = jnp.zeros((t_dec, kt.shape[1]), F32)
        for h in range(N_IDX_HEADS):
            sc = sc + s[h * t_dec:(h + 1) * t_dec, :]
        return _sortable_key(sc)

    c0 = pl.multiple_of(j * (pps * PAGE), pps * PAGE)
    key_ref[:, pl.ds(c0, pps * PAGE)] = scores(jnp.concatenate([page_refs[r][0, 0] for r in range(pps)], axis=1))

    @pl.when(j == pl.num_programs(1) - 1)
    def _():
        t_row = lax.broadcasted_iota(I32, (t_dec, 1), 0)
        lane = lax.broadcasted_iota(I32, (1, cb), 1)
        new = jnp.where(lax.broadcasted_iota(I32, (1, PAGE), 1) <= t_row, scores(inew_ref[0]), INT_MIN)
        key_ref[:, past:past + cb] = jnp.concatenate([new, jnp.full((t_dec, cb - PAGE), INT_MIN, I32)], axis=1)

        def count(pred):
            def body(c, acc):
                c0 = pl.multiple_of(c * cb, cb)
                hit = jnp.where(pred(key_ref[:, pl.ds(c0, cb)], c0 + lane), 1, 0)
                return acc + hit
            acc = lax.fori_loop(0, nblk, body, jnp.zeros((t_dec, cb), I32))
            return acc.sum(axis=1, keepdims=True)

        kq = jnp.minimum(past + t_row + 1, topk)
        thr, cut = _select_threshold(count, kq, (t_dec, 1), max(1, (nkeys - 1).bit_length()))

        def bias_body(c, carry):
            c0 = pl.multiple_of(c * cb, cb)
            k = key_ref[:, pl.ds(c0, cb)]
            sel = (k > thr) | ((k == thr) & (c0 + lane <= cut))
            bias_ref[0, :, pl.ds(c0, cb)] = jnp.where(sel, 0.0, MASK_BIAS).astype(F32)
            return carry

        lax.fori_loop(0, nblk, bias_body, 0)


def _dsa_sample_attend_kernel(pt_ref, q_ref, bias_ref, bnew_ref, knew_ref, vnew_ref, *rest, pps, t_dec):
    k_refs = rest[:pps]
    v_refs = rest[pps:2 * pps]
    o_ref, m_ref, l_ref, acc_ref = rest[2 * pps:]
    j = pl.program_id(1)
    kvw = N_KV_A * HEAD_DIM

    @pl.when(j == 0)
    def _():
        m_ref[...] = jnp.full_like(m_ref, M_INIT)
        l_ref[...] = jnp.zeros_like(l_ref)
        acc_ref[...] = jnp.zeros_like(acc_ref)

    def attend(kt, vt, bias):
        bias4 = jnp.concatenate([bias] * GROUP_A, axis=0)
        for c in range(N_KV_A):
            lg = jnp.dot(q_ref[0, c], kt, preferred_element_type=F32) + bias4
            m = m_ref[c]
            m_new = jnp.maximum(m, lg.max(axis=1, keepdims=True))
            alpha = jnp.exp2(m - m_new)
            p = jnp.exp2(lg - m_new)
            l_ref[c] = alpha * l_ref[c] + p.sum(axis=1, keepdims=True)
            acc_ref[c] = alpha * acc_ref[c] + _dot_nt(p.astype(BF16), vt)
            m_ref[c] = m_new

    kt = jnp.concatenate([k_refs[r][0, 0].reshape(kvw, PAGE) for r in range(pps)], axis=1).astype(BF16)
    vt = jnp.concatenate([v_refs[r][0, 0].reshape(kvw, PAGE) for r in range(pps)], axis=1).astype(BF16)
    attend(kt, vt, bias_ref[0])

    @pl.when(j == pl.num_programs(1) - 1)
    def _():
        attend(knew_ref[0].astype(BF16), vnew_ref[0].astype(BF16), bnew_ref[0])
        for c in range(N_KV_A):
            o_ref[0, c] = acc_ref[c] / l_ref[c]


def dsa_sample(qa, ka, va, qi, ki, wi, pool_k, pool_v, pool_ik, layer, page_table, topk, pps=16, cb=512):
    db, t = qa.shape[:2]
    n_pages = page_table.shape[1]
    past = n_pages * PAGE
    while n_pages % pps:
        pps //= 2
    cb = min(cb, pps * PAGE)
    nsteps = n_pages // pps
    kvw = N_KV_A * HEAD_DIM
    pk, pv, pik = _keys_on_lanes(pool_k), _keys_on_lanes(pool_v), _keys_on_lanes(pool_ik)

    def page_specs(block):
        zeros = (0,) * (len(block) - 2)
        return [pl.BlockSpec(block, lambda b, j, pt, r=r: (layer, pt[b, j * pps + r]) + zeros) for r in range(pps)]

    hrows = N_IDX_HEADS * t
    qi_r = qi.transpose(0, 2, 1, 3).reshape(db, hrows, IDX_DIM).astype(BF16)
    wi_r = wi.transpose(0, 2, 1).reshape(db, hrows, 1)
    nkeys = past + cb
    bias = pl.pallas_call(
        functools.partial(_dsa_sample_select_kernel, pps=pps, t_dec=t, topk=topk, past=past, cb=cb),
        grid_spec=pltpu.PrefetchScalarGridSpec(
            num_scalar_prefetch=1,
            grid=(db, nsteps),
            in_specs=[pl.BlockSpec((1, hrows, IDX_DIM), lambda b, j, pt: (b, 0, 0)),
                      pl.BlockSpec((1, hrows, 1), lambda b, j, pt: (b, 0, 0)),
                      pl.BlockSpec((1, IDX_DIM, PAGE), lambda b, j, pt: (b, 0, 0))]
                     + page_specs((1, 1, IDX_DIM, PAGE)),
            out_specs=pl.BlockSpec((1, t, nkeys), lambda b, j, pt: (b, 0, 0)),
            scratch_shapes=[pltpu.VMEM((t, nkeys), I32)],
        ),
        out_shape=jax.ShapeDtypeStruct((db, t, nkeys), F32),
        compiler_params=_params(("parallel", "arbitrary")),
        name="dsa_sample_select",
    )(page_table, qi_r, wi_r, _new_keys_on_lanes(ki), *([pik] * pps))

    q5 = (qa * (HEAD_DIM ** -0.5 * LOG2E)).reshape(db, t, N_KV_A, GROUP_A, HEAD_DIM).transpose(0, 2, 3, 1, 4)
    q5 = q5.reshape(db, N_KV_A, GROUP_A * t, HEAD_DIM)
    eye = jnp.eye(N_KV_A, dtype=F32)
    q_pad = (q5[:, :, :, None, :] * eye[None, :, None, :, None]).reshape(db, N_KV_A, GROUP_A * t, kvw).astype(BF16)
    rows = GROUP_A * t
    kv_block = (1, 1, N_KV_A, HEAD_DIM, PAGE)
    out = pl.pallas_call(
        functools.partial(_dsa_sample_attend_kernel, pps=pps, t_dec=t),
        grid_spec=pltpu.PrefetchScalarGridSpec(
            num_scalar_prefetch=1,
            grid=(db, nsteps),
            in_specs=[pl.BlockSpec((1, N_KV_A, rows, kvw), lambda b, j, pt: (b, 0, 0, 0)),
                      pl.BlockSpec((1, t, pps * PAGE), lambda b, j, pt: (b, 0, j)),
                      pl.BlockSpec((1, t, PAGE), lambda b, j, pt: (b, 0, past // PAGE)),
                      pl.BlockSpec((1, kvw, PAGE), lambda b, j, pt: (b, 0, 0)),
                      pl.BlockSpec((1, kvw, PAGE), lambda b, j, pt: (b, 0, 0))]
                     + page_specs(kv_block) + page_specs(kv_block),
            out_specs=pl.BlockSpec((1, N_KV_A, rows, kvw), lambda b, j, pt: (b, 0, 0, 0)),
            scratch_shapes=[pltpu.VMEM((N_KV_A, rows, 1), F32), pltpu.VMEM((N_KV_A, rows, 1), F32),
                            pltpu.VMEM((N_KV_A, rows, kvw), F32)],
        ),
        out_shape=jax.ShapeDtypeStruct((db, N_KV_A, rows, kvw), F32),
        compiler_params=_params(("parallel", "arbitrary")),
        name="dsa_sample_attend",
    )(page_table, q_pad, bias, bias, _new_keys_on_lanes(ka.reshape(db, t, kvw)), _new_keys_on_lanes(va.reshape(db, t, kvw)),
      *([pk] * pps), *([pv] * pps))
    o = out.reshape(db, N_KV_A, GROUP_A, t, N_KV_A, HEAD_DIM)
    o = jnp.stack([o[:, c, :, :, c, :] for c in range(N_KV_A)], axis=1)
    return o.transpose(0, 3, 1, 2, 4).reshape(db, t, N_HEADS_A * HEAD_DIM)


def _mem_kernel(x_ref, wq_ref, mk_ref, mv_ref, wo_ref, o_ref):
    x = x_ref[...].reshape(x_ref.shape[-2:]).astype(BF16)
    mk = mk_ref[...].reshape(mk_ref.shape[-2:]).astype(BF16)
    mv = mv_ref[...].reshape(mv_ref.shape[-2:]).astype(BF16)
    q = jnp.dot(x, wq_ref[...], preferred_element_type=F32).astype(BF16)
    outs = []
    for h in range(N_MEM_HEADS):
        sl = slice(h * MEM_HEAD_DIM, (h + 1) * MEM_HEAD_DIM)
        lg = _dot_nt(q[:, sl], mk[:, sl]) * (MEM_HEAD_DIM ** -0.5)
        p = jnp.exp(lg - lg.max(axis=1, keepdims=True))
        p = p / p.sum(axis=1, keepdims=True)
        outs.append(jnp.dot(p.astype(BF16), mv[:, sl], preferred_element_type=F32))
    o = jnp.concatenate(outs, axis=1).astype(BF16)
    o_ref[...] = jnp.dot(o, wo_ref[...], preferred_element_type=F32).reshape(o_ref.shape)


def mem_attend(x, mk, mv, wq, wo, tm):
    b, t, d = x.shape
    n_mem, w = mk.shape[1:]
    return pl.pallas_call(
        _mem_kernel,
        grid=(b, t // tm),
        in_specs=[pl.BlockSpec((1, tm, d), lambda bi, i: (bi, i, 0)),
                  pl.BlockSpec(wq.shape, lambda bi, i: (0, 0)),
                  pl.BlockSpec((1, n_mem, w), lambda bi, i: (bi, 0, 0)),
                  pl.BlockSpec((1, n_mem, w), lambda bi, i: (bi, 0, 0)),
                  pl.BlockSpec(wo.shape, lambda bi, i: (0, 0))],
        out_specs=pl.BlockSpec((1, tm, d), lambda bi, i: (bi, i, 0)),
        out_shape=jax.ShapeDtypeStruct((b, t, d), F32),
        compiler_params=_params(("parallel", "parallel")),
        name="mem_attend",
    )(x, wq, mk, mv, wo)


def _split3_nt(w, x):
    wh = w.astype(BF16)
    wl = (w - wh.astype(F32)).astype(BF16)
    xh = x.astype(BF16)
    xl = (x - xh.astype(F32)).astype(BF16)
    return _dot_nt(wh, xh) + (_dot_nt(wh, xl) + _dot_nt(wl, xh))


def _router_kernel(x_ref, w_ref, b_ref, o_ref):
    logits = _split3_nt(w_ref[...], x_ref[...]) + b_ref[...]
    rows = [logits[e:e + 1, :] for e in range(N_EXPERTS)]
    mx = functools.reduce(jnp.maximum, rows)
    ex = [jnp.exp(r - mx) for r in rows]
    tot = functools.reduce(lambda a, b: a + b, ex)
    p = [e / tot for e in ex]
    gscore = []
    for g in range(N_GROUPS):
        a, b, c, d = p[4 * g:4 * g + 4]
        h1, l1, h2, l2 = jnp.maximum(a, b), jnp.minimum(a, b), jnp.maximum(c, d), jnp.minimum(c, d)
        gscore.append(jnp.maximum(h1, h2) + jnp.maximum(jnp.minimum(h1, h2), jnp.maximum(l1, l2)))
    best = gscore[0]
    group = jnp.zeros_like(best, dtype=I32)
    for g in range(1, N_GROUPS):
        better = gscore[g] > best
        best = jnp.where(better, gscore[g], best)
        group = jnp.where(better, g, group)
    sel = []
    for e in range(N_EXPERTS):
        g = e // EXPERTS_PER_GROUP
        rank = jnp.zeros_like(group)
        for o in range(g * EXPERTS_PER_GROUP, (g + 1) * EXPERTS_PER_GROUP):
            if o != e:
                ahead = (p[o] >= p[e]) if o < e else (p[o] > p[e])
                rank = rank + jnp.where(ahead, 1, 0)
        sel.append((group == g) & (rank < 2))
    top_sum = functools.reduce(lambda a, b: a + b, [jnp.where(s, pe, 0.0) for s, pe in zip(sel, p)])
    o_ref[...] = jnp.concatenate([jnp.where(s, pe / top_sum, 0.0) for s, pe in zip(sel, p)], axis=0)


def router(x, w_router_t, b_router, tm):
    n, d = x.shape
    return pl.pallas_call(
        _router_kernel,
        grid=(n // tm,),
        in_specs=[pl.BlockSpec((tm, d), lambda i: (i, 0)),
                  pl.BlockSpec((N_EXPERTS, d), lambda i: (0, 0)),
                  pl.BlockSpec((N_EXPERTS, 1), lambda i: (0, 0))],
        out_specs=pl.BlockSpec((N_EXPERTS, tm), lambda i: (0, i)),
        out_shape=jax.ShapeDtypeStruct((N_EXPERTS, n), F32),
        compiler_params=_params(("parallel",)),
        name="router",
    )(x, w_router_t, b_router.reshape(N_EXPERTS, 1))


def _moe_kernel(x_ref, wg_ref, wu_ref, wd_ref, comb_ref, o_ref):
    e = pl.program_id(1)
    x = x_ref[...].astype(BF16)
    g = jnp.dot(x, wg_ref[0], preferred_element_type=F32)
    u = jnp.dot(x, wu_ref[0], preferred_element_type=F32)
    h = (g * jax.nn.sigmoid(g)) * u
    y = jnp.dot(h.astype(BF16), wd_ref[0], preferred_element_type=F32)
    lane = lax.broadcasted_iota(I32, (1, N_EXPERTS), 1)
    ce = jnp.sum(jnp.where(lane == e, comb_ref[...], 0.0), axis=1, keepdims=True)

    @pl.when(e == 0)
    def _():
        o_ref[...] = ce * y

    @pl.when(e > 0)
    def _():
        o_ref[...] += ce * y


def moe_experts(x, comb, wg, wu, wd, tm):
    n, d = x.shape
    de = wg.shape[2]
    return pl.pallas_call(
        _moe_kernel,
        grid=(n // tm, N_EXPERTS),
        in_specs=[pl.BlockSpec((tm, d), lambda i, e: (i, 0)),
                  pl.BlockSpec((1, d, de), lambda i, e: (e, 0, 0)),
                  pl.BlockSpec((1, d, de), lambda i, e: (e, 0, 0)),
                  pl.BlockSpec((1, de, d), lambda i, e: (e, 0, 0)),
                  pl.BlockSpec((tm, N_EXPERTS), lambda i, e: (i, 0))],
        out_specs=pl.BlockSpec((tm, d), lambda i, e: (i, 0)),
        out_shape=jax.ShapeDtypeStruct((n, d), F32),
        compiler_params=_params(("parallel", "arbitrary")),
        name="moe_experts",
    )(x, wg, wu, wd, comb)


def _layer_norm(x, g, b):
    mu = x.mean(-1, keepdims=True)
    var = jnp.square(x - mu).mean(-1, keepdims=True)
    return (x - mu) * lax.rsqrt(var + LN_EPS) * g + b


def _rope(x, pos):
    rot = x.shape[-1] // 4
    half = rot // 2
    inv_freq = jnp.power(ROPE_THETA, -jnp.arange(half, dtype=F32) * 2.0 / rot)
    ang = pos.astype(F32)[:, None] * inv_freq[None, :]
    cos = jnp.cos(ang)[:, None, :]
    sin = jnp.sin(ang)[:, None, :]
    x1, x2 = x[..., :half], x[..., half:rot]
    return jnp.concatenate([x1 * cos - x2 * sin, x2 * cos + x1 * sin, x[..., rot:]], axis=-1)


def _pack_w_in(w_in_l):
    cols = [w_in_l[:, _SRC[n][0]:_SRC[n][0] + _SRC[n][1]] for n in _ORDER]
    packed = jnp.concatenate(cols, axis=1)
    return jnp.pad(packed, ((0, 0), (0, W_IN_PACKED - packed.shape[1]))).astype(BF16)


def _take(proj, name):
    o, n = _DST[name]
    return proj[..., o:o + n]


def _mixer_inputs(proj, pos):
    lead = proj.shape[:-1]
    hd = lambda name, h: _take(proj, name).reshape(lead + (h, HEAD_DIM))
    rp = lambda a: _rope(a, pos)
    qa, ka, va = rp(hd("qa", N_HEADS_A)), rp(hd("ka", N_KV_A)), hd("va", N_KV_A)
    qi = rp(hd("qi", N_IDX_HEADS))
    ki = rp(_take(proj, "ki")[..., None, :])[..., 0, :]
    wi = _take(proj, "wi")
    return qa, ka, va, qi, ki, wi, hd("kb", N_HEADS_B), hd("vb", N_HEADS_B)


def _tail(x2, mix, mem_fn, lw, tm):
    alpha = lw["alpha"]
    x2 = _layer_norm(alpha * x2 + mix, lw["ln1_g"], lw["ln1_b"])
    x2 = _layer_norm(alpha * x2 + mem_fn(x2), lw["ln2_g"], lw["ln2_b"])
    comb = router(x2, lw["w_router_t"], lw["b_router"], tm).T
    y = moe_experts(x2, comb, lw["w_gate"], lw["w_up"], lw["w_down"], tm)
    return _layer_norm(alpha * x2 + y, lw["ln3_g"], lw["ln3_b"])


def kernel(x_prompt, x_sample, cache_a_k, cache_a_v, cache_idx_k, cache_b_k, cache_b_v, cache_mem_k, cache_mem_v,
           page_table, mem_prompt, w_in, w_pa, w_pb, w_o, ln1_g, ln1_b, w_cq, w_ck, w_cv, w_co, ln2_g, ln2_b,
           w_router, b_router, w_gate, w_up, w_down, ln3_g, ln3_b):
    depth = w_in.shape[0]
    bp, seq, d = x_prompt.shape
    db, t_dec, _ = x_sample.shape
    n_mem = mem_prompt.shape[1]
    past = page_table.shape[1] * PAGE
    k_prompt = min(MAX_SELECT, seq // 4)
    k_sample = min(MAX_SELECT, (past + t_dec) // 4)
    pos_p = jnp.arange(seq)
    pos_s = past + jnp.arange(t_dec)
    alpha = (2 * depth) ** 0.25
    tm_p = min(1024, seq)
    n_s = db * t_dec
    wm = N_MEM_HEADS * MEM_HEAD_DIM

    xp = x_prompt.reshape(bp * seq, d)
    xs = x_sample.reshape(n_s, d)
    outs = {k: [] for k in ("p_ak", "p_av", "p_ik", "p_bk", "p_bv", "p_mk", "p_mv", "s_ak", "s_av", "s_ik", "s_bk", "s_bv")}
    mem_bf = mem_prompt.reshape(bp * n_mem, d).astype(BF16)
    for l in range(depth):
        lw = dict(alpha=alpha, ln1_g=ln1_g[l], ln1_b=ln1_b[l], ln2_g=ln2_g[l], ln2_b=ln2_b[l],
                  ln3_g=ln3_g[l], ln3_b=ln3_b[l], w_router_t=w_router.T, b_router=b_router,
                  w_gate=w_gate[l].astype(BF16), w_up=w_up[l].astype(BF16), w_down=w_down[l].astype(BF16))
        w_in_p = _pack_w_in(w_in[l])
        wpa, wpb, wo = w_pa[l].astype(BF16), w_pb[l].astype(BF16), w_o[l].astype(BF16)
        wcq, wco = w_cq[l].astype(BF16), w_co[l].astype(BF16)
        wckv = jnp.concatenate([w_ck[l], w_cv[l]], axis=1).astype(BF16)

        proj = matmul(xp.astype(BF16), w_in_p, tm_p, 1024)
        proj_b = proj.reshape(bp, seq, W_IN_PACKED)
        qa, ka, va, qi, ki, wi, kb, vb = _mixer_inputs(proj_b, pos_p)
        ya = jnp.concatenate([dsa_prompt(qa[b], ka[b], va[b], qi[b], ki[b], wi[b], k_prompt) for b in range(bp)], axis=0)
        qb_s = (_take(proj_b, "qb") * (HEAD_DIM ** -0.5)).astype(BF16)
        kb_s = _take(proj_b, "kb").astype(BF16)
        vb_s = _take(proj_b, "vb").astype(BF16)
        yb = jnp.concatenate([sb_prompt(qb_s[b], kb_s[b], vb_s[b]) for b in range(bp)], axis=0)
        mix = gated_merge(ya, yb, proj, wpa, wpb, wo, tm_p)
        mkv = matmul(mem_bf, wckv, bp * n_mem, wm)
        mk, mv = mkv[:, :wm].reshape(bp, n_mem, wm), mkv[:, wm:].reshape(bp, n_mem, wm)
        mem_fn = lambda x2: mem_attend(x2.reshape(bp, seq, d), mk, mv, wcq, wco, tm_p).reshape(bp * seq, d)
        xp = _tail(xp, mix, mem_fn, lw, tm_p)
        outs["p_ak"].append(ka); outs["p_av"].append(va); outs["p_ik"].append(ki)
        outs["p_bk"].append(kb); outs["p_bv"].append(vb)
        outs["p_mk"].append(mk.reshape(bp, n_mem, N_MEM_HEADS, MEM_HEAD_DIM))
        outs["p_mv"].append(mv.reshape(bp, n_mem, N_MEM_HEADS, MEM_HEAD_DIM))

        proj = matmul(xs.astype(BF16), w_in_p, n_s, 1024)
        proj_b = proj.reshape(db, t_dec, W_IN_PACKED)
        qa, ka, va, qi, ki, wi, kb, vb = _mixer_inputs(proj_b, pos_s)
        ya = dsa_sample(qa, ka, va, qi, ki, wi, cache_a_k, cache_a_v, cache_idx_k, l, page_table, k_sample)
        yb = sb_sample(_take(proj_b, "qb"), _take(proj_b, "kb"), _take(proj_b, "vb"), cache_b_k, cache_b_v, l, page_table)
        mix = gated_merge(ya.reshape(n_s, -1), yb.reshape(n_s, -1), proj, wpa, wpb, wo, n_s)
        cmk = cache_mem_k[l].reshape(db, n_mem, wm)
        cmv = cache_mem_v[l].reshape(db, n_mem, wm)
        mem_fn = lambda x2: mem_attend(x2.reshape(db, t_dec, d), cmk, cmv, wcq, wco, t_dec).reshape(n_s, d)
        xs = _tail(xs, mix, mem_fn, lw, n_s)
        outs["s_ak"].append(ka); outs["s_av"].append(va); outs["s_ik"].append(ki)
        outs["s_bk"].append(kb); outs["s_bv"].append(vb)

    st = lambda k: jnp.stack(outs[k])
    return (xp.reshape(bp, seq, d), xs.reshape(db, t_dec, d),
            st("p_ak"), st("p_av"), st("p_ik"), st("p_bk"), st("p_bv"), st("p_mk"), st("p_mv"),
            st("s_ak"), st("s_av"), st("s_ik"), st("s_bk"), st("s_bv"))
```

```python
import functools

import jax
import jax.numpy as jnp
from jax import lax
from jax.experimental import pallas as pl
from jax.experimental.pallas import tpu as pltpu

F32 = jnp.float32
BF16 = jnp.bfloat16
I32 = jnp.int32
I16 = jnp.int16

HEAD_DIM = 64
N_HEADS_A = 8
N_KV_A = 2
GROUP_A = N_HEADS_A // N_KV_A
N_IDX_HEADS = 8
IDX_DIM = 64
N_HEADS_B = 8
MAX_SELECT = 256
ROPE_THETA = 500000.0
PAGE = 128
N_MEM_HEADS = 4
MEM_HEAD_DIM = 128
N_EXPERTS = 16
N_GROUPS = 4
EXPERTS_PER_GROUP = 4
LN_EPS = 1e-5

LANES = 128
SUBLANES = 8
VMEM_LIMIT = 56 * 2**20
N_ACC = 4

INT_MIN = -(2**31)
INT_MAX = 2**31 - 1
M_INIT = -1e30
MASK_BIAS = -2e30
LOG2E = 1.4426950408889634
HALF_BITS = 16
HALF_ROWS = 64
LOW_GROUP = 3
LAZY_SUM_LIMIT = 2.0 ** 30
SB_DEAD = -104.0

_SRC = dict(qa=(0, 512), ka=(512, 128), va=(640, 128), qi=(768, 512), ki=(1280, 64), wi=(1344, 8),
            qb=(1352, 512), kb=(1864, 512), vb=(2376, 512), ga=(2888, 1024), gb=(3912, 1024))
_ORDER = ("ga", "gb", "qa", "qi", "qb", "kb", "vb", "ka", "va", "ki", "wi")
_DST = {}
_off = 0
for _n in _ORDER:
    _DST[_n] = (_off, _SRC[_n][1])
    _off += _SRC[_n][1]
W_IN_PACKED = 5120


def _params(sem):
    return pltpu.CompilerParams(dimension_semantics=sem, vmem_limit_bytes=VMEM_LIMIT)


def _dot_nt(a, b):
    return lax.dot_general(a, b, (((1,), (1,)), ((), ())), preferred_element_type=F32)


def _mm_kernel(x_ref, w_ref, o_ref):
    o_ref[...] = jnp.dot(x_ref[...], w_ref[...], preferred_element_type=F32).astype(o_ref.dtype)


def matmul(x, w, tm, tn, out_dtype=F32):
    m, k = x.shape
    n = w.shape[1]
    return pl.pallas_call(
        _mm_kernel,
        grid=(m // tm, n // tn),
        in_specs=[pl.BlockSpec((tm, k), lambda i, j: (i, 0)), pl.BlockSpec((k, tn), lambda i, j: (0, j))],
        out_specs=pl.BlockSpec((tm, tn), lambda i, j: (i, j)),
        out_shape=jax.ShapeDtypeStruct((m, n), out_dtype),
        compiler_params=_params(("parallel", "arbitrary")),
        name="matmul",
    )(x, w)


def _merge_kernel(ya_ref, yb_ref, ga_ref, gb_ref, wpa_ref, wpb_ref, wo_ref, o_ref):
    a = jnp.dot(ya_ref[...].astype(BF16), wpa_ref[...], preferred_element_type=F32)
    b = jnp.dot(yb_ref[...].astype(BF16), wpb_ref[...], preferred_element_type=F32)
    merged = jax.nn.sigmoid(ga_ref[...]) * a + jax.nn.sigmoid(gb_ref[...]) * b
    o_ref[...] = jnp.dot(merged.astype(BF16), wo_ref[...], preferred_element_type=F32)


def gated_merge(ya, yb, proj, wpa, wpb, wo, tm):
    m = ya.shape[0]
    d = wo.shape[1]
    full = lambda a: pl.BlockSpec(a.shape, lambda i: (0, 0))
    return pl.pallas_call(
        _merge_kernel,
        grid=(m // tm,),
        in_specs=[pl.BlockSpec((tm, ya.shape[1]), lambda i: (i, 0)),
                  pl.BlockSpec((tm, yb.shape[1]), lambda i: (i, 0)),
                  pl.BlockSpec((tm, d), lambda i: (i, _DST["ga"][0] // d)),
                  pl.BlockSpec((tm, d), lambda i: (i, _DST["gb"][0] // d)),
                  full(wpa), full(wpb), full(wo)],
        out_specs=pl.BlockSpec((tm, d), lambda i: (i, 0)),
        out_shape=jax.ShapeDtypeStruct((m, d), F32),
        compiler_params=_params(("parallel",)),
        name="gated_merge",
    )(ya, yb, proj, proj, wpa, wpb, wo)


def _sortable_key(score):
    bits = lax.bitcast_convert_type(score, I32)
    key = jnp.where(bits < 0, bits ^ INT_MAX, bits)
    return jnp.where(key == -1, 0, key)


def _radix_select(count_ge, kq, width, group, shape, found, sep, full_key):
    low = -(1 << (width - 1))

    def probe(cand, prefix, found, sep):
        cnt = count_ge(cand)
        sep = jnp.where((cnt == kq) & (found == 0), full_key(cand), sep)
        return jnp.where(cnt >= kq, cand, prefix), jnp.where(cnt == kq, 1, found), sep

    prefix, found, sep = probe(jnp.zeros(shape, I32), jnp.full(shape, low, I32), found, sep)
    n_groups = -(-(width - 1) // group)

    def group_cond(st):
        return (st[0] < n_groups) & (st[4] > 0)

    def group_body(st):
        gi, prefix, found, sep, _ = st
        for j in range(group):
            shift = width - 2 - (gi * group + j)
            bit = jnp.where(shift >= 0, jnp.left_shift(jnp.int32(1), jnp.maximum(shift, 0)), 0)
            prefix, found, sep = probe(prefix | bit, prefix, found, sep)
        return gi + 1, prefix, found, sep, jnp.max(1 - found)

    _, prefix, found, sep, pending = lax.while_loop(
        group_cond, group_body, (jnp.int32(0), prefix, found, sep, jnp.max(1 - found)))
    return prefix, found, sep, pending


def _tie_cut(count, thr, kq, found, pending, shape, idx_bits):
    def tie_search():
        need = kq - count(lambda k, p: k > thr)

        def tie_body(bi, cut):
            cand = cut | jnp.left_shift(jnp.int32(1), idx_bits - 1 - bi)
            below = count(lambda k, p: (k == thr) & (p < cand))
            return jnp.where(below < need, cand, cut)

        cut = lax.fori_loop(0, idx_bits, tie_body, jnp.zeros(shape, I32))
        return jnp.where(found > 0, INT_MAX, cut)

    return lax.cond(pending > 0, tie_search, lambda: jnp.full(shape, INT_MAX, I32))


def _dsa_prompt_kernel(kidx_ref, qi_ref, wi_ref, k_ref, vt_ref, q_ref, o_ref, key_ref, half_ref, *, tq, kc, sub, topk, idx_bits):
    i = pl.program_id(0)
    nchunk = ((i + 1) * tq + kc - 1) // kc
    qpos = i * tq + lax.broadcasted_iota(I32, (1, tq), 1)
    row_iota = lax.broadcasted_iota(I32, (kc, 1), 0)

    qi = qi_ref[0]
    wi = wi_ref[0]

    def idx_body(c, carry):
        for j in range(kc // sub):
            r0 = pl.multiple_of(c * kc + j * sub, sub)
            s = jnp.dot(kidx_ref[pl.ds(r0, sub), :], qi, preferred_element_type=F32)
            sc = jnp.zeros((sub, tq), F32)
            for h in range(N_IDX_HEADS):
                sc = sc + jnp.maximum(s[:, h * tq:(h + 1) * tq], 0.0) * wi[h:h + 1, :]
            key = jnp.where(r0 + row_iota[:sub] <= qpos, _sortable_key(sc), INT_MIN)
            key_ref[pl.ds(r0, sub), :] = key
            half_ref[pl.ds(r0, sub), :] = (key >> HALF_BITS).astype(I16)
        return carry

    lax.fori_loop(0, nchunk, idx_body, 0)

    def count(pred):
        def body(c, acc):
            r0 = pl.multiple_of(c * kc, kc)
            hit = jnp.where(pred(key_ref[pl.ds(r0, kc), :], r0 + row_iota), 1, 0)
            return acc + hit.reshape(kc // (N_ACC * SUBLANES), N_ACC * SUBLANES, tq).sum(axis=0)
        acc = lax.fori_loop(0, nchunk, body, jnp.zeros((N_ACC * SUBLANES, tq), I32))
        return acc.sum(axis=0, keepdims=True)

    def count_half(cand):
        c16 = cand.astype(I16)

        def body(c, acc):
            r0 = pl.multiple_of(c * kc, kc)
            hit = jnp.where(half_ref[pl.ds(r0, kc), :] >= c16, jnp.int16(1), jnp.int16(0))
            for j in range(kc // HALF_ROWS):
                acc = acc + hit[j * HALF_ROWS:(j + 1) * HALF_ROWS]
            return acc
        acc = lax.fori_loop(0, nchunk, body, jnp.zeros((HALF_ROWS, tq), I16))
        return acc.astype(I32).sum(axis=0, keepdims=True)

    kq = jnp.minimum(qpos + 1, topk)
    none = jnp.zeros((1, tq), I32)
    t_hi, found, sep, pending = _radix_select(
        count_half, kq, HALF_BITS, HALF_BITS - 1, (1, tq), none, none, lambda c: jnp.left_shift(c, HALF_BITS))
    half_low = -(1 << (HALF_BITS - 1))

    def low_level():
        def build(c, carry):
            r0 = pl.multiple_of(c * kc, kc)
            k = key_ref[pl.ds(r0, kc), :]
            hi = k >> HALF_BITS
            lo = (k & ((1 << HALF_BITS) - 1)) + half_low
            pinned = jnp.where(hi > t_hi, -half_low - 1, half_low)
            half_ref[pl.ds(r0, kc), :] = jnp.where(hi == t_hi, lo, pinned).astype(I16)
            return carry

        lax.fori_loop(0, nchunk, build, 0)
        base = jnp.left_shift(t_hi, HALF_BITS) - half_low
        t_lo, found2, sep2, pending2 = _radix_select(
            count_half, kq, HALF_BITS, LOW_GROUP, (1, tq), found, sep, lambda c: base + c)
        return base + t_lo, found2, sep2, pending2

    thr, found, sep, pending = lax.cond(pending > 0, low_level, lambda: (none, found, sep, pending))
    thr = jnp.where(found > 0, sep, thr)
    cut = _tie_cut(count, thr, kq, found, pending, (1, tq), idx_bits)

    def bias_body(c, carry):
        r0 = pl.multiple_of(c * kc, kc)
        k = key_ref[pl.ds(r0, kc), :]
        sel = (k > thr) | ((k == thr) & (r0 + row_iota <= cut))
        key_ref[pl.ds(r0, kc), :] = lax.bitcast_convert_type(jnp.where(sel, 0.0, MASK_BIAS).astype(F32), I32)
        return carry

    lax.fori_loop(0, nchunk, bias_body, 0)

    nq = GROUP_A * tq

    def att_body(ch, carry):
        def logits(c, r0, n):
            bias = lax.bitcast_convert_type(key_ref[pl.ds(r0, n), :], F32)
            bias = jnp.concatenate([bias] * GROUP_A, axis=1)
            return jnp.dot(k_ref[pl.ds(r0, n), :], q_ref[0, c], preferred_element_type=F32) + bias

        def pv(c, p, r0, n):
            vt = vt_ref[c * HEAD_DIM:(c + 1) * HEAD_DIM, pl.ds(r0, n)]
            return jnp.dot(vt, p.astype(BF16), preferred_element_type=F32)

        r0 = pl.multiple_of(ch * kc, kc)
        lazy, overrun = [], jnp.float32(0.0)
        lgs = [logits(c, r0, kc) for c in range(N_KV_A)]
        for c in range(N_KV_A):
            m, l, acc = carry[c]
            p = jnp.exp2(lgs[c] - m)
            l = l + p.sum(axis=0, keepdims=True)
            lazy.append((m, l, acc + pv(c, p, r0, kc)))
            overrun = jnp.maximum(overrun, jnp.max(jnp.where(l < LAZY_SUM_LIMIT, 0.0, 1.0)))

        def keep_lazy():
            return tuple(lazy)

        def redo_with_chunk_max():
            out = []
            for c in range(N_KV_A):
                m, l, acc = carry[c]
                lg = logits(c, r0, kc)
                m_new = jnp.maximum(m, lg.max(axis=0, keepdims=True))
                alpha = jnp.exp2(m - m_new)
                p = jnp.exp2(lg - m_new)
                out.append((m_new, alpha * l + p.sum(axis=0, keepdims=True), alpha * acc + pv(c, p, r0, kc)))
            return tuple(out)

        return lax.cond(overrun > 0.0, redo_with_chunk_max, keep_lazy)

    init = (jnp.full((1, nq), M_INIT, F32), jnp.zeros((1, nq), F32), jnp.zeros((HEAD_DIM, nq), F32))
    res = lax.fori_loop(0, nchunk, att_body, (init,) * N_KV_A)
    for c in range(N_KV_A):
        o_ref[0, c] = res[c][2] / res[c][1]


def dsa_prompt(qa, ka, va, qi, ki, wi, topk, tq=128, kc=1024, sub=128):
    s = qa.shape[0]
    kc = min(kc, s)
    nt = s // tq
    kidx = ki.astype(BF16)
    qi_t = qi.reshape(nt, tq, N_IDX_HEADS, IDX_DIM).transpose(0, 3, 2, 1).reshape(nt, IDX_DIM, N_IDX_HEADS * tq).astype(BF16)
    wi_t = wi.reshape(nt, tq, N_IDX_HEADS).transpose(0, 2, 1)
    k2 = ka.reshape(s, N_KV_A * HEAD_DIM).astype(BF16)
    v2t = va.reshape(s, N_KV_A * HEAD_DIM).T.astype(BF16)
    q5 = (qa * (HEAD_DIM ** -0.5 * LOG2E)).reshape(nt, tq, N_KV_A, GROUP_A, HEAD_DIM).transpose(0, 2, 4, 3, 1)
    q5 = q5.reshape(nt, N_KV_A, HEAD_DIM, GROUP_A * tq)
    eye = jnp.eye(N_KV_A, dtype=F32)
    q_pad = (eye[None, :, :, None, None] * q5[:, :, None, :, :]).reshape(nt, N_KV_A, N_KV_A * HEAD_DIM, GROUP_A * tq)
    q_pad = q_pad.astype(BF16)
    out = pl.pallas_call(
        functools.partial(_dsa_prompt_kernel, tq=tq, kc=kc, sub=min(sub, kc), topk=topk, idx_bits=max(1, (s - 1).bit_length())),
        grid=(nt,),
        in_specs=[pl.BlockSpec((s, IDX_DIM), lambda i: (0, 0)),
                  pl.BlockSpec((1, IDX_DIM, N_IDX_HEADS * tq), lambda i: (i, 0, 0)),
                  pl.BlockSpec((1, N_IDX_HEADS, tq), lambda i: (i, 0, 0)),
                  pl.BlockSpec((s, N_KV_A * HEAD_DIM), lambda i: (0, 0)),
                  pl.BlockSpec((N_KV_A * HEAD_DIM, s), lambda i: (0, 0)),
                  pl.BlockSpec((1, N_KV_A, N_KV_A * HEAD_DIM, GROUP_A * tq), lambda i: (i, 0, 0, 0))],
        out_specs=pl.BlockSpec((1, N_KV_A, HEAD_DIM, GROUP_A * tq), lambda i: (i, 0, 0, 0)),
        out_shape=jax.ShapeDtypeStruct((nt, N_KV_A, HEAD_DIM, GROUP_A * tq), F32),
        scratch_shapes=[pltpu.VMEM((s, tq), I32), pltpu.VMEM((s, tq), I16)],
        compiler_params=_params(("parallel",)),
        name="dsa_prompt",
    )(kidx, qi_t, wi_t, k2, v2t, q_pad)
    out = out.reshape(nt, N_KV_A, HEAD_DIM, GROUP_A, tq).transpose(0, 4, 1, 3, 2)
    return out.reshape(s, N_HEADS_A * HEAD_DIM)


def _tri(n):
    return jnp.where(lax.broadcasted_iota(I32, (n, n), 0) > lax.broadcasted_iota(I32, (n, n), 1), 1.0, 0.0).astype(BF16)


def _sb_blocks(zs, masks, vs, carries, tri, v_keys_on_lanes=False):
    n = len(zs)
    log_beta, log_keep, parts = [], [], []
    for z, mask in zip(zs, masks):
        soft = jnp.log1p(jnp.exp(-jnp.abs(z)))
        log_beta.append(jnp.minimum(z, 0.0) - soft)
        lk = -jnp.maximum(z, 0.0) - soft
        if mask is not None:
            lk = jnp.where(mask, lk, 0.0)
        log_keep.append(lk)
        hi = lk.astype(BF16)
        r1 = lk - hi.astype(F32)
        mid = r1.astype(BF16)
        parts.append((hi, mid, (r1 - mid.astype(F32)).astype(BF16)))
    after = [(jnp.dot(hi, tri, preferred_element_type=F32) + jnp.dot(mid, tri, preferred_element_type=F32)
              + jnp.dot(lo, tri, preferred_element_type=F32)) for hi, mid, lo in parts]
    weights = []
    for i in range(n):
        a = jnp.exp(log_beta[i] + (after[i] + carries[i]))
        if masks[i] is not None:
            a = jnp.where(masks[i], a, 0.0)
        weights.append(a.astype(BF16))
    outs = [_dot_nt(a, v) if v_keys_on_lanes else jnp.dot(a, v, preferred_element_type=F32)
            for a, v in zip(weights, vs)]
    return outs, [c + jnp.sum(lk, axis=1, keepdims=True) for c, lk in zip(carries, log_keep)]


def _sb_block(z, mask, v, carry, tri, v_keys_on_lanes=False):
    outs, carries = _sb_blocks([z], [mask], [v], [carry], tri, v_keys_on_lanes)
    return outs[0], carries[0]


def _sb_prompt_kernel(q_ref, k_ref, v_ref, o_ref, *, tq, nsub):
    i = pl.program_id(1)
    lane = lax.broadcasted_iota(I32, (1, 2 * HEAD_DIM), 1)
    row = lax.broadcasted_iota(I32, (tq, 1), 0)
    key_iota = lax.broadcasted_iota(I32, (1, tq), 1)
    tri = _tri(tq)
    q_all = q_ref[...]
    qh = [[jnp.where(lane // HEAD_DIM == hh, q_all[s * tq:(s + 1) * tq], jnp.zeros((tq, 2 * HEAD_DIM), q_all.dtype))
           for hh in range(2)] for s in range(nsub)]

    def cond(st):
        return st[1] > 0

    def body(st):
        d, _, carries, accs = st
        zs, masks, vs, older = [], [], [], []
        for s in range(nsub):
            tile = i * nsub + s
            c = tile - d
            r0 = pl.multiple_of(jnp.maximum(c, 0) * tq, tq)
            kblk = k_ref[pl.ds(r0, tq), :]
            mask = ((r0 + key_iota) < (tile * tq + row)) & (c >= 0)
            for hh in range(2):
                zs.append(_dot_nt(qh[s][hh], kblk))
                masks.append(mask)
                vs.append(v_ref[pl.ds(r0, tq), :])
                older.append(c >= 1)
        outs, new_c = _sb_blocks(zs, masks, vs, list(carries), tri)
        live = jnp.full((tq, 1), SB_DEAD, F32)
        for carry, more in zip(new_c, older):
            live = jnp.maximum(live, jnp.where(more, carry, SB_DEAD))
        alive = (jnp.max(live) > SB_DEAD).astype(I32)
        return d + 1, alive, tuple(new_c), tuple(a + o for a, o in zip(accs, outs))

    zc = jnp.zeros((tq, 1), F32)
    za = jnp.zeros((tq, 2 * HEAD_DIM), F32)
    accs = lax.while_loop(cond, body, (jnp.int32(0), jnp.int32(1), (zc,) * (2 * nsub), (za,) * (2 * nsub)))[3]
    for s in range(nsub):
        o_ref[s * tq:(s + 1) * tq, :] = jnp.where(lane < HEAD_DIM, accs[2 * s], accs[2 * s + 1])


def sb_prompt(q, k, v, tq=128, nsub=4):
    s, w = k.shape
    npair = w // (2 * HEAD_DIM)
    nsub = min(nsub, s // tq)
    rows = nsub * tq
    return pl.pallas_call(
        functools.partial(_sb_prompt_kernel, tq=tq, nsub=nsub),
        grid=(npair, s // rows),
        in_specs=[pl.BlockSpec((rows, 2 * HEAD_DIM), lambda h, i: (i, h)),
                  pl.BlockSpec((s, 2 * HEAD_DIM), lambda h, i: (0, h)),
                  pl.BlockSpec((s, 2 * HEAD_DIM), lambda h, i: (0, h))],
        out_specs=pl.BlockSpec((rows, 2 * HEAD_DIM), lambda h, i: (i, h)),
        out_shape=jax.ShapeDtypeStruct((s, w), F32),
        compiler_params=_params(("parallel", "parallel")),
        name="sb_prompt",
    )(q, k, v)


def _keys_on_lanes(pool):
    nd = pool.ndim
    return pool.transpose((0, 1) + tuple(range(3, nd)) + (2,))


def _new_keys_on_lanes(a):
    a = a.transpose(0, 2, 1)
    return jnp.pad(a, ((0, 0), (0, 0), (0, PAGE - a.shape[2])))


def _sb_sample_kernel(pt_ref, flag_ref, q_ref, a_ref, b_ref, *rest, pps, t_dec, first):
    k_refs = rest[:pps]
    v_refs = rest[pps:2 * pps]
    outs = rest[2 * pps:]
    rows = N_HEADS_B * t_dec
    w = N_HEADS_B * HEAD_DIM
    q = q_ref[0]
    tri = _tri(PAGE)
    if first:
        acc_ref, carry_ref, alive_ref = outs
        acc_o, carry_o = acc_ref.at[0], carry_ref.at[0]
    else:
        o_ref, acc_o, carry_o, alive_ref = outs
    j = pl.program_id(1)

    def block(kt, vt, mask):
        z = jnp.dot(q, kt.astype(BF16), preferred_element_type=F32)
        out, carry = _sb_block(z, mask, vt.astype(BF16), carry_o[...], tri, v_keys_on_lanes=True)
        carry_o[...] = carry
        acc_o[...] += out
        alive_ref[0] = (jnp.max(carry) > SB_DEAD).astype(I32)

    @pl.when(j == 0)
    def _():
        if first:
            carry_o[...] = jnp.zeros((rows, 1), F32)
            acc_o[...] = jnp.zeros((rows, w), F32)
            t_row = lax.broadcasted_iota(I32, (rows, 1), 0) % t_dec
            block(a_ref[0], b_ref[0], lax.broadcasted_iota(I32, (1, PAGE), 1) < t_row)
        else:
            acc_o[...] = a_ref[0]
            carry_o[...] = b_ref[0]
            alive_ref[0] = flag_ref[pl.program_id(0)]

    for r in range(pps):
        @pl.when(alive_ref[0] > 0)
        def _(r=r):
            block(k_refs[r][0, 0].reshape(w, PAGE), v_refs[r][0, 0].reshape(w, PAGE), None)

    if not first:
        @pl.when(j == pl.num_programs(1) - 1)
        def _():
            lane = lax.broadcasted_iota(I32, (1, w), 1)
            acc = acc_o[...]
            out = jnp.zeros((t_dec, w), F32)
            for h in range(N_HEADS_B):
                out = out + jnp.where(lane // HEAD_DIM == h, acc[h * t_dec:(h + 1) * t_dec, :], 0.0)
            o_ref[0] = out


def _block_diag_rows(q, n_heads):
    b, t, w = q.shape
    d = w // n_heads
    sel = (jnp.arange(w)[None, :] // d == jnp.arange(n_heads)[:, None]).astype(q.dtype)
    return (q[:, None, :, :] * sel[None, :, None, :]).reshape(b, n_heads * t, w)


def sb_sample(qb, kb, vb, pool_k, pool_v, layer, page_table, head_pages=8, pps=24):
    db, t, w = qb.shape
    n_pages = page_table.shape[1]
    head_pages = min(head_pages, n_pages)
    rest = n_pages - head_pages
    while rest % pps:
        pps -= 1
    pk, pv = _keys_on_lanes(pool_k), _keys_on_lanes(pool_v)
    q_bd = _block_diag_rows(qb * (HEAD_DIM ** -0.5), N_HEADS_B).astype(BF16)
    rows = N_HEADS_B * t
    page_block = (1, 1, N_HEADS_B, HEAD_DIM, PAGE)
    batch3 = lambda shape: pl.BlockSpec((1,) + shape, lambda b, j, pt, fl: (b, 0, 0))

    def head_spec(r):
        return pl.BlockSpec(page_block, lambda b, j, pt, fl: (layer, pt[b, n_pages - 1 - r], 0, 0, 0))

    ones = jnp.ones((db,), I32)
    acc, carry = pl.pallas_call(
        functools.partial(_sb_sample_kernel, pps=head_pages, t_dec=t, first=True),
        grid_spec=pltpu.PrefetchScalarGridSpec(
            num_scalar_prefetch=2,
            grid=(db, 1),
            in_specs=[batch3((rows, w)), batch3((w, PAGE)), batch3((w, PAGE))]
                     + [head_spec(r) for r in range(head_pages)] * 2,
            out_specs=[batch3((rows, w)), batch3((rows, 1))],
            scratch_shapes=[pltpu.SMEM((1,), I32)],
        ),
        out_shape=[jax.ShapeDtypeStruct((db, rows, w), F32), jax.ShapeDtypeStruct((db, rows, 1), F32)],
        compiler_params=_params(("parallel", "arbitrary")),
        name="sb_sample_head",
    )(page_table, ones, q_bd, _new_keys_on_lanes(kb), _new_keys_on_lanes(vb), *([pk] * head_pages), *([pv] * head_pages))

    alive = (jnp.max(carry, axis=(1, 2)) > SB_DEAD).astype(I32)

    def tail_spec(r):
        def index(b, j, pt, fl):
            page = pt[b, n_pages - 1 - head_pages - (j * pps + r)]
            return (layer, jnp.where(fl[b] > 0, page, 0), 0, 0, 0)
        return pl.BlockSpec(page_block, index)

    return pl.pallas_call(
        functools.partial(_sb_sample_kernel, pps=pps, t_dec=t, first=False),
        grid_spec=pltpu.PrefetchScalarGridSpec(
            num_scalar_prefetch=2,
            grid=(db, rest // pps),
            in_specs=[batch3((rows, w)), batch3((rows, w)), batch3((rows, 1))]
                     + [tail_spec(r) for r in range(pps)] * 2,
            out_specs=batch3((t, w)),
            scratch_shapes=[pltpu.VMEM((rows, w), F32), pltpu.VMEM((rows, 1), F32), pltpu.SMEM((1,), I32)],
        ),
        out_shape=jax.ShapeDtypeStruct((db, t, w), F32),
        compiler_params=_params(("parallel", "arbitrary")),
        name="sb_sample_tail",
    )(page_table, alive, q_bd, acc, carry, *([pk] * pps), *([pv] * pps))


def _dsa_sample_select_kernel(pt_ref, qi_ref, wi_ref, inew_ref, *rest, pps, t_dec, topk, past, cb):
    page_refs = rest[:pps]
    bias_ref, key_ref = rest[pps:]
    j = pl.program_id(1)
    qi = qi_ref[0]
    wi = wi_ref[0]
    nkeys = past + cb
    nblk = nkeys // cb

    def scores(kt):
        s = jnp.maximum(jnp.dot(qi, kt.astype(BF16), preferred_element_type=F32), 0.0) * wi
        sc = jnp.zeros((t_dec, kt.shape[1]), F32)
        for h in range(N_IDX_HEADS):
            sc = sc + s[h * t_dec:(h + 1) * t_dec, :]
        return _sortable_key(sc)

    c0 = pl.multiple_of(j * (pps * PAGE), pps * PAGE)
    key_ref[:, pl.ds(c0, pps * PAGE)] = scores(jnp.concatenate([page_refs[r][0, 0] for r in range(pps)], axis=1))

    @pl.when(j == pl.num_programs(1) - 1)
    def _():
        t_row = lax.broadcasted_iota(I32, (t_dec, 1), 0)
        lane = lax.broadcasted_iota(I32, (1, cb), 1)
        new = jnp.where(lax.broadcasted_iota(I32, (1, PAGE), 1) <= t_row, scores(inew_ref[0]), INT_MIN)
        key_ref[:, past:past + cb] = jnp.concatenate([new, jnp.full((t_dec, cb - PAGE), INT_MIN, I32)], axis=1)

        def count(pred):
            def body(c, acc):
                c0 = pl.multiple_of(c * cb, cb)
                hit = jnp.where(pred(key_ref[:, pl.ds(c0, cb)], c0 + lane), 1, 0)
                return acc + hit
            acc = lax.fori_loop(0, nblk, body, jnp.zeros((t_dec, cb), I32))
            return acc.sum(axis=1, keepdims=True)

        kq = jnp.minimum(past + t_row + 1, topk)
        none = jnp.zeros((t_dec, 1), I32)
        thr, found, sep, pending = _radix_select(
            lambda cand: count(lambda k, p: k >= cand), kq, 32, 4, (t_dec, 1), none, none, lambda c: c)
        thr = jnp.where(found > 0, sep, thr)
        cut = _tie_cut(count, thr, kq, found, pending, (t_dec, 1), max(1, (nkeys - 1).bit_length()))

        def bias_body(c, carry):
            c0 = pl.multiple_of(c * cb, cb)
            k = key_ref[:, pl.ds(c0, cb)]
            sel = (k > thr) | ((k == thr) & (c0 + lane <= cut))
            bias_ref[0, :, pl.ds(c0, cb)] = jnp.where(sel, 0.0, MASK_BIAS).astype(F32)
            return carry

        lax.fori_loop(0, nblk, bias_body, 0)


def _dsa_sample_attend_kernel(pt_ref, q_ref, bias_ref, bnew_ref, knew_ref, vnew_ref, *rest, pps, t_dec):
    k_refs = rest[:pps]
    v_refs = rest[pps:2 * pps]
    o_ref, m_ref, l_ref, acc_ref = rest[2 * pps:]
    j = pl.program_id(1)
    kvw = N_KV_A * HEAD_DIM

    @pl.when(j == 0)
    def _():
        m_ref[...] = jnp.full_like(m_ref, M_INIT)
        l_ref[...] = jnp.zeros_like(l_ref)
        acc_ref[...] = jnp.zeros_like(acc_ref)

    def attend(kt, vt, bias):
        bias4 = jnp.concatenate([bias] * GROUP_A, axis=0)
        for c in range(N_KV_A):
            lg = jnp.dot(q_ref[0, c], kt, preferred_element_type=F32) + bias4
            m = m_ref[c]
            m_new = jnp.maximum(m, lg.max(axis=1, keepdims=True))
            alpha = jnp.exp2(m - m_new)
            p = jnp.exp2(lg - m_new)
            l_ref[c] = alpha * l_ref[c] + p.sum(axis=1, keepdims=True)
            acc_ref[c] = alpha * acc_ref[c] + _dot_nt(p.astype(BF16), vt)
            m_ref[c] = m_new

    kt = jnp.concatenate([k_refs[r][0, 0].reshape(kvw, PAGE) for r in range(pps)], axis=1).astype(BF16)
    vt = jnp.concatenate([v_refs[r][0, 0].reshape(kvw, PAGE) for r in range(pps)], axis=1).astype(BF16)
    attend(kt, vt, bias_ref[0])

    @pl.when(j == pl.num_programs(1) - 1)
    def _():
        attend(knew_ref[0].astype(BF16), vnew_ref[0].astype(BF16), bnew_ref[0])
        for c in range(N_KV_A):
            o_ref[0, c] = acc_ref[c] / l_ref[c]


def dsa_sample(qa, ka, va, qi, ki, wi, pool_k, pool_v, pool_ik, layer, page_table, topk, pps=16, cb=512):
    db, t = qa.shape[:2]
    n_pages = page_table.shape[1]
    past = n_pages * PAGE
    while n_pages % pps:
        pps //= 2
    cb = min(cb, pps * PAGE)
    nsteps = n_pages // pps
    kvw = N_KV_A * HEAD_DIM
    pk, pv, pik = _keys_on_lanes(pool_k), _keys_on_lanes(pool_v), _keys_on_lanes(pool_ik)

    def page_specs(block):
        zeros = (0,) * (len(block) - 2)
        return [pl.BlockSpec(block, lambda b, j, pt, r=r: (layer, pt[b, j * pps + r]) + zeros) for r in range(pps)]

    hrows = N_IDX_HEADS * t
    qi_r = qi.transpose(0, 2, 1, 3).reshape(db, hrows, IDX_DIM).astype(BF16)
    wi_r = wi.transpose(0, 2, 1).reshape(db, hrows, 1)
    nkeys = past + cb
    bias = pl.pallas_call(
        functools.partial(_dsa_sample_select_kernel, pps=pps, t_dec=t, topk=topk, past=past, cb=cb),
        grid_spec=pltpu.PrefetchScalarGridSpec(
            num_scalar_prefetch=1,
            grid=(db, nsteps),
            in_specs=[pl.BlockSpec((1, hrows, IDX_DIM), lambda b, j, pt: (b, 0, 0)),
                      pl.BlockSpec((1, hrows, 1), lambda b, j, pt: (b, 0, 0)),
                      pl.BlockSpec((1, IDX_DIM, PAGE), lambda b, j, pt: (b, 0, 0))]
                     + page_specs((1, 1, IDX_DIM, PAGE)),
            out_specs=pl.BlockSpec((1, t, nkeys), lambda b, j, pt: (b, 0, 0)),
            scratch_shapes=[pltpu.VMEM((t, nkeys), I32)],
        ),
        out_shape=jax.ShapeDtypeStruct((db, t, nkeys), F32),
        compiler_params=_params(("parallel", "arbitrary")),
        name="dsa_sample_select",
    )(page_table, qi_r, wi_r, _new_keys_on_lanes(ki), *([pik] * pps))

    q5 = (qa * (HEAD_DIM ** -0.5 * LOG2E)).reshape(db, t, N_KV_A, GROUP_A, HEAD_DIM).transpose(0, 2, 3, 1, 4)
    q5 = q5.reshape(db, N_KV_A, GROUP_A * t, HEAD_DIM)
    eye = jnp.eye(N_KV_A, dtype=F32)
    q_pad = (q5[:, :, :, None, :] * eye[None, :, None, :, None]).reshape(db, N_KV_A, GROUP_A * t, kvw).astype(BF16)
    rows = GROUP_A * t
    kv_block = (1, 1, N_KV_A, HEAD_DIM, PAGE)
    out = pl.pallas_call(
        functools.partial(_dsa_sample_attend_kernel, pps=pps, t_dec=t),
        grid_spec=pltpu.PrefetchScalarGridSpec(
            num_scalar_prefetch=1,
            grid=(db, nsteps),
            in_specs=[pl.BlockSpec((1, N_KV_A, rows, kvw), lambda b, j, pt: (b, 0, 0, 0)),
                      pl.BlockSpec((1, t, pps * PAGE), lambda b, j, pt: (b, 0, j)),
                      pl.BlockSpec((1, t, PAGE), lambda b, j, pt: (b, 0, past // PAGE)),
                      pl.BlockSpec((1, kvw, PAGE), lambda b, j, pt: (b, 0, 0)),
                      pl.BlockSpec((1, kvw, PAGE), lambda b, j, pt: (b, 0, 0))]
                     + page_specs(kv_block) + page_specs(kv_block),
            out_specs=pl.BlockSpec((1, N_KV_A, rows, kvw), lambda b, j, pt: (b, 0, 0, 0)),
            scratch_shapes=[pltpu.VMEM((N_KV_A, rows, 1), F32), pltpu.VMEM((N_KV_A, rows, 1), F32),
                            pltpu.VMEM((N_KV_A, rows, kvw), F32)],
        ),
        out_shape=jax.ShapeDtypeStruct((db, N_KV_A, rows, kvw), F32),
        compiler_params=_params(("parallel", "arbitrary")),
        name="dsa_sample_attend",
    )(page_table, q_pad, bias, bias, _new_keys_on_lanes(ka.reshape(db, t, kvw)), _new_keys_on_lanes(va.reshape(db, t, kvw)),
      *([pk] * pps), *([pv] * pps))
    o = out.reshape(db, N_KV_A, GROUP_A, t, N_KV_A, HEAD_DIM)
    o = jnp.stack([o[:, c, :, :, c, :] for c in range(N_KV_A)], axis=1)
    return o.transpose(0, 3, 1, 2, 4).reshape(db, t, N_HEADS_A * HEAD_DIM)


def _mem_kernel(x_ref, wq_ref, mk_ref, mv_ref, wo_ref, o_ref):
    x = x_ref[...].reshape(x_ref.shape[-2:]).astype(BF16)
    mk = mk_ref[...].reshape(mk_ref.shape[-2:]).astype(BF16)
    mv = mv_ref[...].reshape(mv_ref.shape[-2:]).astype(BF16)
    q = jnp.dot(x, wq_ref[...], preferred_element_type=F32).astype(BF16)
    outs = []
    for h in range(N_MEM_HEADS):
        sl = slice(h * MEM_HEAD_DIM, (h + 1) * MEM_HEAD_DIM)
        lg = _dot_nt(q[:, sl], mk[:, sl]) * (MEM_HEAD_DIM ** -0.5)
        p = jnp.exp(lg - lg.max(axis=1, keepdims=True))
        p = p / p.sum(axis=1, keepdims=True)
        outs.append(jnp.dot(p.astype(BF16), mv[:, sl], preferred_element_type=F32))
    o = jnp.concatenate(outs, axis=1).astype(BF16)
    o_ref[...] = jnp.dot(o, wo_ref[...], preferred_element_type=F32).reshape(o_ref.shape)


def mem_attend(x, mk, mv, wq, wo, tm):
    b, t, d = x.shape
    n_mem, w = mk.shape[1:]
    return pl.pallas_call(
        _mem_kernel,
        grid=(b, t // tm),
        in_specs=[pl.BlockSpec((1, tm, d), lambda bi, i: (bi, i, 0)),
                  pl.BlockSpec(wq.shape, lambda bi, i: (0, 0)),
                  pl.BlockSpec((1, n_mem, w), lambda bi, i: (bi, 0, 0)),
                  pl.BlockSpec((1, n_mem, w), lambda bi, i: (bi, 0, 0)),
                  pl.BlockSpec(wo.shape, lambda bi, i: (0, 0))],
        out_specs=pl.BlockSpec((1, tm, d), lambda bi, i: (bi, i, 0)),
        out_shape=jax.ShapeDtypeStruct((b, t, d), F32),
        compiler_params=_params(("parallel", "parallel")),
        name="mem_attend",
    )(x, wq, mk, mv, wo)


def _split3_nt(w, x):
    wh = w.astype(BF16)
    wl = (w - wh.astype(F32)).astype(BF16)
    xh = x.astype(BF16)
    xl = (x - xh.astype(F32)).astype(BF16)
    return _dot_nt(wh, xh) + (_dot_nt(wh, xl) + _dot_nt(wl, xh))


def _router_kernel(x_ref, w_ref, b_ref, o_ref):
    logits = _split3_nt(w_ref[...], x_ref[...]) + b_ref[...]
    rows = [logits[e:e + 1, :] for e in range(N_EXPERTS)]
    mx = functools.reduce(jnp.maximum, rows)
    ex = [jnp.exp(r - mx) for r in rows]
    tot = functools.reduce(lambda a, b: a + b, ex)
    p = [e / tot for e in ex]
    gscore = []
    for g in range(N_GROUPS):
        a, b, c, d = p[4 * g:4 * g + 4]
        h1, l1, h2, l2 = jnp.maximum(a, b), jnp.minimum(a, b), jnp.maximum(c, d), jnp.minimum(c, d)
        gscore.append(jnp.maximum(h1, h2) + jnp.maximum(jnp.minimum(h1, h2), jnp.maximum(l1, l2)))
    best = gscore[0]
    group = jnp.zeros_like(best, dtype=I32)
    for g in range(1, N_GROUPS):
        better = gscore[g] > best
        best = jnp.where(better, gscore[g], best)
        group = jnp.where(better, g, group)
    sel = []
    for e in range(N_EXPERTS):
        g = e // EXPERTS_PER_GROUP
        rank = jnp.zeros_like(group)
        for o in range(g * EXPERTS_PER_GROUP, (g + 1) * EXPERTS_PER_GROUP):
            if o != e:
                ahead = (p[o] >= p[e]) if o < e else (p[o] > p[e])
                rank = rank + jnp.where(ahead, 1, 0)
        sel.append((group == g) & (rank < 2))
    top_sum = functools.reduce(lambda a, b: a + b, [jnp.where(s, pe, 0.0) for s, pe in zip(sel, p)])
    o_ref[...] = jnp.concatenate([jnp.where(s, pe / top_sum, 0.0) for s, pe in zip(sel, p)], axis=0)


def router(x, w_router_t, b_router, tm):
    n, d = x.shape
    return pl.pallas_call(
        _router_kernel,
        grid=(n // tm,),
        in_specs=[pl.BlockSpec((tm, d), lambda i: (i, 0)),
                  pl.BlockSpec((N_EXPERTS, d), lambda i: (0, 0)),
                  pl.BlockSpec((N_EXPERTS, 1), lambda i: (0, 0))],
        out_specs=pl.BlockSpec((N_EXPERTS, tm), lambda i: (0, i)),
        out_shape=jax.ShapeDtypeStruct((N_EXPERTS, n), F32),
        compiler_params=_params(("parallel",)),
        name="router",
    )(x, w_router_t, b_router.reshape(N_EXPERTS, 1))


def _moe_kernel(x_ref, wg_ref, wu_ref, wd_ref, comb_ref, o_ref):
    e = pl.program_id(1)
    x = x_ref[...].astype(BF16)
    g = jnp.dot(x, wg_ref[0], preferred_element_type=F32)
    u = jnp.dot(x, wu_ref[0], preferred_element_type=F32)
    h = (g * jax.nn.sigmoid(g)) * u
    y = jnp.dot(h.astype(BF16), wd_ref[0], preferred_element_type=F32)
    lane = lax.broadcasted_iota(I32, (1, N_EXPERTS), 1)
    ce = jnp.sum(jnp.where(lane == e, comb_ref[...], 0.0), axis=1, keepdims=True)

    @pl.when(e == 0)
    def _():
        o_ref[...] = ce * y

    @pl.when(e > 0)
    def _():
        o_ref[...] += ce * y


def moe_experts(x, comb, wg, wu, wd, tm):
    n, d = x.shape
    de = wg.shape[2]
    return pl.pallas_call(
        _moe_kernel,
        grid=(n // tm, N_EXPERTS),
        in_specs=[pl.BlockSpec((tm, d), lambda i, e: (i, 0)),
                  pl.BlockSpec((1, d, de), lambda i, e: (e, 0, 0)),
                  pl.BlockSpec((1, d, de), lambda i, e: (e, 0, 0)),
                  pl.BlockSpec((1, de, d), lambda i, e: (e, 0, 0)),
                  pl.BlockSpec((tm, N_EXPERTS), lambda i, e: (i, 0))],
        out_specs=pl.BlockSpec((tm, d), lambda i, e: (i, 0)),
        out_shape=jax.ShapeDtypeStruct((n, d), F32),
        compiler_params=_params(("parallel", "arbitrary")),
        name="moe_experts",
    )(x, wg, wu, wd, comb)


def _layer_norm(x, g, b):
    mu = x.mean(-1, keepdims=True)
    var = jnp.square(x - mu).mean(-1, keepdims=True)
    return (x - mu) * lax.rsqrt(var + LN_EPS) * g + b


def _rope(x, pos):
    rot = x.shape[-1] // 4
    half = rot // 2
    inv_freq = jnp.power(ROPE_THETA, -jnp.arange(half, dtype=F32) * 2.0 / rot)
    ang = pos.astype(F32)[:, None] * inv_freq[None, :]
    cos = jnp.cos(ang)[:, None, :]
    sin = jnp.sin(ang)[:, None, :]
    x1, x2 = x[..., :half], x[..., half:rot]
    return jnp.concatenate([x1 * cos - x2 * sin, x2 * cos + x1 * sin, x[..., rot:]], axis=-1)


def _pack_w_in(w_in_l):
    cols = [w_in_l[:, _SRC[n][0]:_SRC[n][0] + _SRC[n][1]] for n in _ORDER]
    packed = jnp.concatenate(cols, axis=1)
    return jnp.pad(packed, ((0, 0), (0, W_IN_PACKED - packed.shape[1]))).astype(BF16)


def _take(proj, name):
    o, n = _DST[name]
    return proj[..., o:o + n]


def _mixer_inputs(proj, pos):
    lead = proj.shape[:-1]
    hd = lambda name, h: _take(proj, name).reshape(lead + (h, HEAD_DIM))
    rp = lambda a: _rope(a, pos)
    qa, ka, va = rp(hd("qa", N_HEADS_A)), rp(hd("ka", N_KV_A)), hd("va", N_KV_A)
    qi = rp(hd("qi", N_IDX_HEADS))
    ki = rp(_take(proj, "ki")[..., None, :])[..., 0, :]
    wi = _take(proj, "wi")
    return qa, ka, va, qi, ki, wi, hd("kb", N_HEADS_B), hd("vb", N_HEADS_B)


def _tail(x2, mix, mem_fn, lw, tm):
    alpha = lw["alpha"]
    x2 = _layer_norm(alpha * x2 + mix, lw["ln1_g"], lw["ln1_b"])
    x2 = _layer_norm(alpha * x2 + mem_fn(x2), lw["ln2_g"], lw["ln2_b"])
    comb = router(x2, lw["w_router_t"], lw["b_router"], tm).T
    y = moe_experts(x2, comb, lw["w_gate"], lw["w_up"], lw["w_down"], tm)
    return _layer_norm(alpha * x2 + y, lw["ln3_g"], lw["ln3_b"])


def kernel(x_prompt, x_sample, cache_a_k, cache_a_v, cache_idx_k, cache_b_k, cache_b_v, cache_mem_k, cache_mem_v,
           page_table, mem_prompt, w_in, w_pa, w_pb, w_o, ln1_g, ln1_b, w_cq, w_ck, w_cv, w_co, ln2_g, ln2_b,
           w_router, b_router, w_gate, w_up, w_down, ln3_g, ln3_b):
    depth = w_in.shape[0]
    bp, seq, d = x_prompt.shape
    db, t_dec, _ = x_sample.shape
    n_mem = mem_prompt.shape[1]
    past = page_table.shape[1] * PAGE
    k_prompt = min(MAX_SELECT, seq // 4)
    k_sample = min(MAX_SELECT, (past + t_dec) // 4)
    pos_p = jnp.arange(seq)
    pos_s = past + jnp.arange(t_dec)
    alpha = (2 * depth) ** 0.25
    tm_p = min(1024, seq)
    n_s = db * t_dec
    wm = N_MEM_HEADS * MEM_HEAD_DIM

    xp = x_prompt.reshape(bp * seq, d)
    xs = x_sample.reshape(n_s, d)
    outs = {k: [] for k in ("p_ak", "p_av", "p_ik", "p_bk", "p_bv", "p_mk", "p_mv", "s_ak", "s_av", "s_ik", "s_bk", "s_bv")}
    mem_bf = mem_prompt.reshape(bp * n_mem, d).astype(BF16)
    for l in range(depth):
        lw = dict(alpha=alpha, ln1_g=ln1_g[l], ln1_b=ln1_b[l], ln2_g=ln2_g[l], ln2_b=ln2_b[l],
                  ln3_g=ln3_g[l], ln3_b=ln3_b[l], w_router_t=w_router.T, b_router=b_router,
                  w_gate=w_gate[l].astype(BF16), w_up=w_up[l].astype(BF16), w_down=w_down[l].astype(BF16))
        w_in_p = _pack_w_in(w_in[l])
        wpa, wpb, wo = w_pa[l].astype(BF16), w_pb[l].astype(BF16), w_o[l].astype(BF16)
        wcq, wco = w_cq[l].astype(BF16), w_co[l].astype(BF16)
        wckv = jnp.concatenate([w_ck[l], w_cv[l]], axis=1).astype(BF16)

        proj = matmul(xp.astype(BF16), w_in_p, tm_p, 1024)
        proj_b = proj.reshape(bp, seq, W_IN_PACKED)
        qa, ka, va, qi, ki, wi, kb, vb = _mixer_inputs(proj_b, pos_p)
        ya = jnp.concatenate([dsa_prompt(qa[b], ka[b], va[b], qi[b], ki[b], wi[b], k_prompt) for b in range(bp)], axis=0)
        qb_s = (_take(proj_b, "qb") * (HEAD_DIM ** -0.5)).astype(BF16)
        kb_s = _take(proj_b, "kb").astype(BF16)
        vb_s = _take(proj_b, "vb").astype(BF16)
        yb = jnp.concatenate([sb_prompt(qb_s[b], kb_s[b], vb_s[b]) for b in range(bp)], axis=0)
        mix = gated_merge(ya, yb, proj, wpa, wpb, wo, tm_p)
        mkv = matmul(mem_bf, wckv, bp * n_mem, wm)
        mk, mv = mkv[:, :wm].reshape(bp, n_mem, wm), mkv[:, wm:].reshape(bp, n_mem, wm)
        mem_fn = lambda x2: mem_attend(x2.reshape(bp, seq, d), mk, mv, wcq, wco, tm_p).reshape(bp * seq, d)
        xp = _tail(xp, mix, mem_fn, lw, tm_p)
        outs["p_ak"].append(ka); outs["p_av"].append(va); outs["p_ik"].append(ki)
        outs["p_bk"].append(kb); outs["p_bv"].append(vb)
        outs["p_mk"].append(mk.reshape(bp, n_mem, N_MEM_HEADS, MEM_HEAD_DIM))
        outs["p_mv"].append(mv.reshape(bp, n_mem, N_MEM_HEADS, MEM_HEAD_DIM))

        proj = matmul(xs.astype(BF16), w_in_p, n_s, 1024)
        proj_b = proj.reshape(db, t_dec, W_IN_PACKED)
        qa, ka, va, qi, ki, wi, kb, vb = _mixer_inputs(proj_b, pos_s)
        ya = dsa_sample(qa, ka, va, qi, ki, wi, cache_a_k, cache_a_v, cache_idx_k, l, page_table, k_sample)
        yb = sb_sample(_take(proj_b, "qb"), _take(proj_b, "kb"), _take(proj_b, "vb"), cache_b_k, cache_b_v, l, page_table)
        mix = gated_merge(ya.reshape(n_s, -1), yb.reshape(n_s, -1), proj, wpa, wpb, wo, n_s)
        cmk = cache_mem_k[l].reshape(db, n_mem, wm)
        cmv = cache_mem_v[l].reshape(db, n_mem, wm)
        mem_fn = lambda x2: mem_attend(x2.reshape(db, t_dec, d), cmk, cmv, wcq, wco, t_dec).reshape(n_s, d)
        xs = _tail(xs, mix, mem_fn, lw, n_s)
        outs["s_ak"].append(ka); outs["s_av"].append(va); outs["s_ik"].append(ki)
        outs["s_bk"].append(kb); outs["s_bv"].append(vb)

    st = lambda k: jnp.stack(outs[k])
    return (xp.reshape(bp, seq, d), xs.reshape(db, t_dec, d),
            st("p_ak"), st("p_av"), st("p_ik"), st("p_bk"), st("p_bv"), st("p_mk"), st("p_mv"),
            st("s_ak"), st("s_av"), st("s_ik"), st("s_bk"), st("s_bv"))
```

```python
import functools

import jax
import jax.numpy as jnp
from jax import lax
from jax.experimental import pallas as pl
from jax.experimental.pallas import tpu as pltpu

F32 = jnp.float32
BF16 = jnp.bfloat16
I32 = jnp.int32

HEAD_DIM = 64
N_HEADS_A = 8
N_KV_A = 2
GROUP_A = N_HEADS_A // N_KV_A
N_IDX_HEADS = 8
IDX_DIM = 64
N_HEADS_B = 8
MAX_SELECT = 256
ROPE_THETA = 500000.0
PAGE = 128
N_MEM_HEADS = 4
MEM_HEAD_DIM = 128
N_EXPERTS = 16
N_GROUPS = 4
EXPERTS_PER_GROUP = 4
LN_EPS = 1e-5

LANES = 128
SUBLANES = 8
VMEM_LIMIT = 56 * 2**20
N_ACC = 4

INT_MIN = -(2**31)
INT_MAX = 2**31 - 1
M_INIT = -1e30
MASK_BIAS = -2e30
NO_LIMIT = 1e9
LOG2E = 1.4426950408889634
RADIX_GROUP = 4
LAZY_SUM_LIMIT = 2.0 ** 30
SB_DEAD = -104.0

_SRC = dict(qa=(0, 512), ka=(512, 128), va=(640, 128), qi=(768, 512), ki=(1280, 64), wi=(1344, 8),
            qb=(1352, 512), kb=(1864, 512), vb=(2376, 512), ga=(2888, 1024), gb=(3912, 1024))
_ORDER = ("ga", "gb", "qa", "qi", "qb", "kb", "vb", "ka", "va", "ki", "wi")
_DST = {}
_off = 0
for _n in _ORDER:
    _DST[_n] = (_off, _SRC[_n][1])
    _off += _SRC[_n][1]
W_IN_PACKED = 5120


def _params(sem):
    return pltpu.CompilerParams(dimension_semantics=sem, vmem_limit_bytes=VMEM_LIMIT)


def _dot_nt(a, b):
    return lax.dot_general(a, b, (((1,), (1,)), ((), ())), preferred_element_type=F32)


def _mm_kernel(x_ref, w_ref, o_ref):
    o_ref[...] = jnp.dot(x_ref[...], w_ref[...], preferred_element_type=F32).astype(o_ref.dtype)


def matmul(x, w, tm, tn, out_dtype=F32):
    m, k = x.shape
    n = w.shape[1]
    return pl.pallas_call(
        _mm_kernel,
        grid=(m // tm, n // tn),
        in_specs=[pl.BlockSpec((tm, k), lambda i, j: (i, 0)), pl.BlockSpec((k, tn), lambda i, j: (0, j))],
        out_specs=pl.BlockSpec((tm, tn), lambda i, j: (i, j)),
        out_shape=jax.ShapeDtypeStruct((m, n), out_dtype),
        compiler_params=_params(("parallel", "arbitrary")),
        name="matmul",
    )(x, w)


def _merge_kernel(ya_ref, yb_ref, ga_ref, gb_ref, wpa_ref, wpb_ref, wo_ref, o_ref):
    a = jnp.dot(ya_ref[...].astype(BF16), wpa_ref[...], preferred_element_type=F32)
    b = jnp.dot(yb_ref[...].astype(BF16), wpb_ref[...], preferred_element_type=F32)
    merged = jax.nn.sigmoid(ga_ref[...]) * a + jax.nn.sigmoid(gb_ref[...]) * b
    o_ref[...] = jnp.dot(merged.astype(BF16), wo_ref[...], preferred_element_type=F32)


def gated_merge(ya, yb, proj, wpa, wpb, wo, tm):
    m = ya.shape[0]
    d = wo.shape[1]
    full = lambda a: pl.BlockSpec(a.shape, lambda i: (0, 0))
    return pl.pallas_call(
        _merge_kernel,
        grid=(m // tm,),
        in_specs=[pl.BlockSpec((tm, ya.shape[1]), lambda i: (i, 0)),
                  pl.BlockSpec((tm, yb.shape[1]), lambda i: (i, 0)),
                  pl.BlockSpec((tm, d), lambda i: (i, _DST["ga"][0] // d)),
                  pl.BlockSpec((tm, d), lambda i: (i, _DST["gb"][0] // d)),
                  full(wpa), full(wpb), full(wo)],
        out_specs=pl.BlockSpec((tm, d), lambda i: (i, 0)),
        out_shape=jax.ShapeDtypeStruct((m, d), F32),
        compiler_params=_params(("parallel",)),
        name="gated_merge",
    )(ya, yb, proj, proj, wpa, wpb, wo)


def _sortable_key(score):
    bits = lax.bitcast_convert_type(score, I32)
    key = jnp.where(bits < 0, bits ^ INT_MAX, bits)
    return jnp.where(key == -1, 0, key)


def _radix_select(count_ge, kq, width, group, shape, found, sep, full_key):
    low = -(1 << (width - 1))

    def probe(cand, prefix, found, sep):
        cnt = count_ge(cand)
        sep = jnp.where((cnt == kq) & (found == 0), full_key(cand), sep)
        return jnp.where(cnt >= kq, cand, prefix), jnp.where(cnt == kq, 1, found), sep

    prefix, found, sep = probe(jnp.zeros(shape, I32), jnp.full(shape, low, I32), found, sep)
    n_groups = -(-(width - 1) // group)

    def group_cond(st):
        return (st[0] < n_groups) & (st[4] > 0)

    def group_body(st):
        gi, prefix, found, sep, _ = st

        def bit_body(j, inner):
            prefix, found, sep = inner
            shift = width - 2 - (gi * group + j)
            bit = jnp.where(shift >= 0, jnp.left_shift(jnp.int32(1), jnp.maximum(shift, 0)), 0)
            return probe(prefix | bit, prefix, found, sep)

        prefix, found, sep = lax.fori_loop(0, group, bit_body, (prefix, found, sep))
        return gi + 1, prefix, found, sep, jnp.max(1 - found)

    _, prefix, found, sep, pending = lax.while_loop(
        group_cond, group_body, (jnp.int32(0), prefix, found, sep, jnp.max(1 - found)))
    return prefix, found, sep, pending


def _tie_cut(count, thr, kq, found, pending, shape, idx_bits):
    def tie_search():
        need = kq - count(lambda k, p: k > thr)

        def tie_body(bi, cut):
            cand = cut | jnp.left_shift(jnp.int32(1), idx_bits - 1 - bi)
            below = count(lambda k, p: (k == thr) & (p < cand))
            return jnp.where(below < need, cand, cut)

        cut = lax.fori_loop(0, idx_bits, tie_body, jnp.zeros(shape, I32))
        return jnp.where(found > 0, INT_MAX, cut)

    return lax.cond(pending > 0, tie_search, lambda: jnp.full(shape, INT_MAX, I32))


def _dsa_prompt_kernel(kidx_ref, qi_ref, wi_ref, k_ref, vt_ref, q_ref, o_ref, key_ref, *, tq, kc, sub, topk):
    i = pl.program_id(0)
    nchunk = ((i + 1) * tq + kc - 1) // kc
    qpos = i * tq + lax.broadcasted_iota(I32, (1, tq), 1)
    row_iota = lax.broadcasted_iota(I32, (kc, 1), 0)

    qi = qi_ref[0]
    wi = wi_ref[0]

    def idx_body(c, carry):
        for j in range(kc // sub):
            r0 = pl.multiple_of(c * kc + j * sub, sub)
            s = jnp.dot(kidx_ref[pl.ds(r0, sub), :], qi, preferred_element_type=F32)
            sc = jnp.zeros((sub, tq), F32)
            for h in range(N_IDX_HEADS):
                sc = sc + jnp.maximum(s[:, h * tq:(h + 1) * tq], 0.0) * wi[h:h + 1, :]
            key = jnp.where(r0 + row_iota[:sub] <= qpos, _sortable_key(sc), INT_MIN)
            key_ref[pl.ds(r0, sub), :] = key
        return carry

    lax.fori_loop(0, nchunk, idx_body, 0)

    def count(pred):
        def body(c, acc):
            r0 = pl.multiple_of(c * kc, kc)
            hit = jnp.where(pred(key_ref[pl.ds(r0, kc), :], r0 + row_iota), 1, 0)
            return acc + hit.reshape(kc // (N_ACC * SUBLANES), N_ACC * SUBLANES, tq).sum(axis=0)
        acc = lax.fori_loop(0, nchunk, body, jnp.zeros((N_ACC * SUBLANES, tq), I32))
        return acc.sum(axis=0, keepdims=True)

    kq = jnp.minimum(qpos + 1, topk)
    none = jnp.zeros((1, tq), I32)
    thr, found, sep, pending = _radix_select(
        lambda cand: count(lambda k, p: k >= cand), kq, 32, RADIX_GROUP, (1, tq), none, none, lambda c: c)
    thr = jnp.where(found > 0, sep, thr)
    n_gt = lax.cond(pending > 0, lambda: count(lambda k, p: k > thr), lambda: none)
    need = jnp.where(found > 0, NO_LIMIT, (kq - n_gt).astype(F32))
    earlier = jnp.where(lax.broadcasted_iota(I32, (sub, sub), 1) < lax.broadcasted_iota(I32, (sub, sub), 0), 1.0, 0.0).astype(BF16)

    def bias_body(c, ties_before):
        for j in range(kc // sub):
            r0 = pl.multiple_of(c * kc + j * sub, sub)
            k = key_ref[pl.ds(r0, sub), :]
            tie = jnp.where(k == thr, 1.0, 0.0)
            before = jnp.dot(earlier, tie.astype(BF16), preferred_element_type=F32) + ties_before
            sel = (k > thr) | ((k == thr) & (before < need))
            key_ref[pl.ds(r0, sub), :] = lax.bitcast_convert_type(jnp.where(sel, 0.0, MASK_BIAS).astype(F32), I32)
            ties_before = ties_before + tie.sum(axis=0, keepdims=True)
        return ties_before

    lax.fori_loop(0, nchunk, bias_body, jnp.zeros((1, tq), F32))

    nq = GROUP_A * tq

    def att_body(ch, carry):
        def logits(c, r0, n):
            bias = lax.bitcast_convert_type(key_ref[pl.ds(r0, n), :], F32)
            bias = jnp.concatenate([bias] * GROUP_A, axis=1)
            return jnp.dot(k_ref[pl.ds(r0, n), :], q_ref[0, c], preferred_element_type=F32) + bias

        def pv(c, p, r0, n):
            vt = vt_ref[c * HEAD_DIM:(c + 1) * HEAD_DIM, pl.ds(r0, n)]
            return jnp.dot(vt, p.astype(BF16), preferred_element_type=F32)

        r0 = pl.multiple_of(ch * kc, kc)
        lazy, overrun = [], jnp.float32(0.0)
        lgs = [logits(c, r0, kc) for c in range(N_KV_A)]
        for c in range(N_KV_A):
            m, l, acc = carry[c]
            p = jnp.exp2(lgs[c] - m)
            l = l + p.sum(axis=0, keepdims=True)
            lazy.append((m, l, acc + pv(c, p, r0, kc)))
            overrun = jnp.maximum(overrun, jnp.max(jnp.where(l < LAZY_SUM_LIMIT, 0.0, 1.0)))

        def keep_lazy():
            return tuple(lazy)

        def redo_with_chunk_max():
            out = []
            for c in range(N_KV_A):
                m, l, acc = carry[c]
                lg = logits(c, r0, kc)
                m_new = jnp.maximum(m, lg.max(axis=0, keepdims=True))
                alpha = jnp.exp2(m - m_new)
                p = jnp.exp2(lg - m_new)
                out.append((m_new, alpha * l + p.sum(axis=0, keepdims=True), alpha * acc + pv(c, p, r0, kc)))
            return tuple(out)

        return lax.cond(overrun > 0.0, redo_with_chunk_max, keep_lazy)

    init = (jnp.full((1, nq), M_INIT, F32), jnp.zeros((1, nq), F32), jnp.zeros((HEAD_DIM, nq), F32))
    res = lax.fori_loop(0, nchunk, att_body, (init,) * N_KV_A)
    for c in range(N_KV_A):
        o_ref[0, c] = res[c][2] / res[c][1]


def dsa_prompt(qa, ka, va, qi, ki, wi, topk, tq=128, kc=1024, sub=128):
    s = qa.shape[0]
    kc = min(kc, s)
    nt = s // tq
    kidx = ki.astype(BF16)
    qi_t = qi.reshape(nt, tq, N_IDX_HEADS, IDX_DIM).transpose(0, 3, 2, 1).reshape(nt, IDX_DIM, N_IDX_HEADS * tq).astype(BF16)
    wi_t = wi.reshape(nt, tq, N_IDX_HEADS).transpose(0, 2, 1)
    k2 = ka.reshape(s, N_KV_A * HEAD_DIM).astype(BF16)
    v2t = va.reshape(s, N_KV_A * HEAD_DIM).T.astype(BF16)
    q5 = (qa * (HEAD_DIM ** -0.5 * LOG2E)).reshape(nt, tq, N_KV_A, GROUP_A, HEAD_DIM).transpose(0, 2, 4, 3, 1)
    q5 = q5.reshape(nt, N_KV_A, HEAD_DIM, GROUP_A * tq)
    eye = jnp.eye(N_KV_A, dtype=F32)
    q_pad = (eye[None, :, :, None, None] * q5[:, :, None, :, :]).reshape(nt, N_KV_A, N_KV_A * HEAD_DIM, GROUP_A * tq)
    q_pad = q_pad.astype(BF16)
    out = pl.pallas_call(
        functools.partial(_dsa_prompt_kernel, tq=tq, kc=kc, sub=min(sub, kc), topk=topk),
        grid=(nt,),
        in_specs=[pl.BlockSpec((s, IDX_DIM), lambda i: (0, 0)),
                  pl.BlockSpec((1, IDX_DIM, N_IDX_HEADS * tq), lambda i: (i, 0, 0)),
                  pl.BlockSpec((1, N_IDX_HEADS, tq), lambda i: (i, 0, 0)),
                  pl.BlockSpec((s, N_KV_A * HEAD_DIM), lambda i: (0, 0)),
                  pl.BlockSpec((N_KV_A * HEAD_DIM, s), lambda i: (0, 0)),
                  pl.BlockSpec((1, N_KV_A, N_KV_A * HEAD_DIM, GROUP_A * tq), lambda i: (i, 0, 0, 0))],
        out_specs=pl.BlockSpec((1, N_KV_A, HEAD_DIM, GROUP_A * tq), lambda i: (i, 0, 0, 0)),
        out_shape=jax.ShapeDtypeStruct((nt, N_KV_A, HEAD_DIM, GROUP_A * tq), F32),
        scratch_shapes=[pltpu.VMEM((s, tq), I32)],
        compiler_params=_params(("parallel",)),
        name="dsa_prompt",
    )(kidx, qi_t, wi_t, k2, v2t, q_pad)
    out = out.reshape(nt, N_KV_A, HEAD_DIM, GROUP_A, tq).transpose(0, 4, 1, 3, 2)
    return out.reshape(s, N_HEADS_A * HEAD_DIM)


def _tri(n):
    return jnp.where(lax.broadcasted_iota(I32, (n, n), 0) > lax.broadcasted_iota(I32, (n, n), 1), 1.0, 0.0).astype(BF16)


def _sb_blocks(zs, masks, vs, carries, tri, v_keys_on_lanes=False):
    n = len(zs)
    log_beta, log_keep, parts = [], [], []
    for z, mask in zip(zs, masks):
        soft = jnp.log1p(jnp.exp(-jnp.abs(z)))
        log_beta.append(jnp.minimum(z, 0.0) - soft)
        lk = -jnp.maximum(z, 0.0) - soft
        if mask is not None:
            lk = jnp.where(mask, lk, 0.0)
        log_keep.append(lk)
        hi = lk.astype(BF16)
        r1 = lk - hi.astype(F32)
        mid = r1.astype(BF16)
        parts.append((hi, mid, (r1 - mid.astype(F32)).astype(BF16)))
    after = [(jnp.dot(hi, tri, preferred_element_type=F32) + jnp.dot(mid, tri, preferred_element_type=F32)
              + jnp.dot(lo, tri, preferred_element_type=F32)) for hi, mid, lo in parts]
    weights = []
    for i in range(n):
        a = jnp.exp(log_beta[i] + (after[i] + carries[i]))
        if masks[i] is not None:
            a = jnp.where(masks[i], a, 0.0)
        weights.append(a.astype(BF16))
    outs = [_dot_nt(a, v) if v_keys_on_lanes else jnp.dot(a, v, preferred_element_type=F32)
            for a, v in zip(weights, vs)]
    return outs, [c + jnp.sum(lk, axis=1, keepdims=True) for c, lk in zip(carries, log_keep)]


def _sb_block(z, mask, v, carry, tri, v_keys_on_lanes=False):
    outs, carries = _sb_blocks([z], [mask], [v], [carry], tri, v_keys_on_lanes)
    return outs[0], carries[0]


def _sb_prompt_kernel(q_ref, k_ref, v_ref, o_ref, *, tq, nsub):
    i = pl.program_id(1)
    lane = lax.broadcasted_iota(I32, (1, 2 * HEAD_DIM), 1)
    row = lax.broadcasted_iota(I32, (tq, 1), 0)
    key_iota = lax.broadcasted_iota(I32, (1, tq), 1)
    tri = _tri(tq)
    q_all = q_ref[...]
    qh = [[jnp.where(lane // HEAD_DIM == hh, q_all[s * tq:(s + 1) * tq], jnp.zeros((tq, 2 * HEAD_DIM), q_all.dtype))
           for hh in range(2)] for s in range(nsub)]

    def cond(st):
        return st[1] > 0

    def body(st):
        d, _, carries, accs = st
        zs, masks, vs, older = [], [], [], []
        for s in range(nsub):
            tile = i * nsub + s
            c = tile - d
            r0 = pl.multiple_of(jnp.maximum(c, 0) * tq, tq)
            kblk = k_ref[pl.ds(r0, tq), :]
            mask = ((r0 + key_iota) < (tile * tq + row)) & (c >= 0)
            for hh in range(2):
                zs.append(_dot_nt(qh[s][hh], kblk))
                masks.append(mask)
                vs.append(v_ref[pl.ds(r0, tq), :])
                older.append(c >= 1)
        outs, new_c = _sb_blocks(zs, masks, vs, list(carries), tri)
        live = jnp.full((tq, 1), SB_DEAD, F32)
        for carry, more in zip(new_c, older):
            live = jnp.maximum(live, jnp.where(more, carry, SB_DEAD))
        alive = (jnp.max(live) > SB_DEAD).astype(I32)
        return d + 1, alive, tuple(new_c), tuple(a + o for a, o in zip(accs, outs))

    zc = jnp.zeros((tq, 1), F32)
    za = jnp.zeros((tq, 2 * HEAD_DIM), F32)
    accs = lax.while_loop(cond, body, (jnp.int32(0), jnp.int32(1), (zc,) * (2 * nsub), (za,) * (2 * nsub)))[3]
    for s in range(nsub):
        o_ref[s * tq:(s + 1) * tq, :] = jnp.where(lane < HEAD_DIM, accs[2 * s], accs[2 * s + 1])


def sb_prompt(q, k, v, tq=128, nsub=4):
    s, w = k.shape
    npair = w // (2 * HEAD_DIM)
    nsub = min(nsub, s // tq)
    rows = nsub * tq
    return pl.pallas_call(
        functools.partial(_sb_prompt_kernel, tq=tq, nsub=nsub),
        grid=(npair, s // rows),
        in_specs=[pl.BlockSpec((rows, 2 * HEAD_DIM), lambda h, i: (i, h)),
                  pl.BlockSpec((s, 2 * HEAD_DIM), lambda h, i: (0, h)),
                  pl.BlockSpec((s, 2 * HEAD_DIM), lambda h, i: (0, h))],
        out_specs=pl.BlockSpec((rows, 2 * HEAD_DIM), lambda h, i: (i, h)),
        out_shape=jax.ShapeDtypeStruct((s, w), F32),
        compiler_params=_params(("parallel", "parallel")),
        name="sb_prompt",
    )(q, k, v)


def _keys_on_lanes(pool):
    nd = pool.ndim
    return pool.transpose((0, 1) + tuple(range(3, nd)) + (2,))


def _new_keys_on_lanes(a):
    a = a.transpose(0, 2, 1)
    return jnp.pad(a, ((0, 0), (0, 0), (0, PAGE - a.shape[2])))


def _sb_sample_kernel(pt_ref, flag_ref, q_ref, a_ref, b_ref, *rest, pps, t_dec, first):
    k_refs = rest[:pps]
    v_refs = rest[pps:2 * pps]
    outs = rest[2 * pps:]
    rows = N_HEADS_B * t_dec
    w = N_HEADS_B * HEAD_DIM
    q = q_ref[0]
    tri = _tri(PAGE)
    if first:
        acc_ref, carry_ref, alive_ref = outs
        acc_o, carry_o = acc_ref.at[0], carry_ref.at[0]
    else:
        o_ref, acc_o, carry_o, alive_ref = outs
    j = pl.program_id(1)

    def block(kt, vt, mask):
        z = jnp.dot(q, kt.astype(BF16), preferred_element_type=F32)
        out, carry = _sb_block(z, mask, vt.astype(BF16), carry_o[...], tri, v_keys_on_lanes=True)
        carry_o[...] = carry
        acc_o[...] += out
        alive_ref[0] = (jnp.max(carry) > SB_DEAD).astype(I32)

    @pl.when(j == 0)
    def _():
        if first:
            carry_o[...] = jnp.zeros((rows, 1), F32)
            acc_o[...] = jnp.zeros((rows, w), F32)
            t_row = lax.broadcasted_iota(I32, (rows, 1), 0) % t_dec
            block(a_ref[0], b_ref[0], lax.broadcasted_iota(I32, (1, PAGE), 1) < t_row)
        else:
            acc_o[...] = a_ref[0]
            carry_o[...] = b_ref[0]
            alive_ref[0] = flag_ref[pl.program_id(0)]

    for r in range(pps):
        @pl.when(alive_ref[0] > 0)
        def _(r=r):
            block(k_refs[r][0, 0].reshape(w, PAGE), v_refs[r][0, 0].reshape(w, PAGE), None)

    if not first:
        @pl.when(j == pl.num_programs(1) - 1)
        def _():
            lane = lax.broadcasted_iota(I32, (1, w), 1)
            acc = acc_o[...]
            out = jnp.zeros((t_dec, w), F32)
            for h in range(N_HEADS_B):
                out = out + jnp.where(lane // HEAD_DIM == h, acc[h * t_dec:(h + 1) * t_dec, :], 0.0)
            o_ref[0] = out


def _block_diag_rows(q, n_heads):
    b, t, w = q.shape
    d = w // n_heads
    sel = (jnp.arange(w)[None, :] // d == jnp.arange(n_heads)[:, None]).astype(q.dtype)
    return (q[:, None, :, :] * sel[None, :, None, :]).reshape(b, n_heads * t, w)


def sb_sample(qb, kb, vb, pool_k, pool_v, layer, page_table, head_pages=8, pps=24):
    db, t, w = qb.shape
    n_pages = page_table.shape[1]
    head_pages = min(head_pages, n_pages)
    rest = n_pages - head_pages
    while rest % pps:
        pps -= 1
    pk, pv = _keys_on_lanes(pool_k), _keys_on_lanes(pool_v)
    q_bd = _block_diag_rows(qb * (HEAD_DIM ** -0.5), N_HEADS_B).astype(BF16)
    rows = N_HEADS_B * t
    page_block = (1, 1, N_HEADS_B, HEAD_DIM, PAGE)
    batch3 = lambda shape: pl.BlockSpec((1,) + shape, lambda b, j, pt, fl: (b, 0, 0))

    def head_spec(r):
        return pl.BlockSpec(page_block, lambda b, j, pt, fl: (layer, pt[b, n_pages - 1 - r], 0, 0, 0))

    ones = jnp.ones((db,), I32)
    acc, carry = pl.pallas_call(
        functools.partial(_sb_sample_kernel, pps=head_pages, t_dec=t, first=True),
        grid_spec=pltpu.PrefetchScalarGridSpec(
            num_scalar_prefetch=2,
            grid=(db, 1),
            in_specs=[batch3((rows, w)), batch3((w, PAGE)), batch3((w, PAGE))]
                     + [head_spec(r) for r in range(head_pages)] * 2,
            out_specs=[batch3((rows, w)), batch3((rows, 1))],
            scratch_shapes=[pltpu.SMEM((1,), I32)],
        ),
        out_shape=[jax.ShapeDtypeStruct((db, rows, w), F32), jax.ShapeDtypeStruct((db, rows, 1), F32)],
        compiler_params=_params(("parallel", "arbitrary")),
        name="sb_sample_head",
    )(page_table, ones, q_bd, _new_keys_on_lanes(kb), _new_keys_on_lanes(vb), *([pk] * head_pages), *([pv] * head_pages))

    alive = (jnp.max(carry, axis=(1, 2)) > SB_DEAD).astype(I32)

    def tail_spec(r):
        def index(b, j, pt, fl):
            page = pt[b, n_pages - 1 - head_pages - (j * pps + r)]
            return (layer, jnp.where(fl[b] > 0, page, 0), 0, 0, 0)
        return pl.BlockSpec(page_block, index)

    return pl.pallas_call(
        functools.partial(_sb_sample_kernel, pps=pps, t_dec=t, first=False),
        grid_spec=pltpu.PrefetchScalarGridSpec(
            num_scalar_prefetch=2,
            grid=(db, rest // pps),
            in_specs=[batch3((rows, w)), batch3((rows, w)), batch3((rows, 1))]
                     + [tail_spec(r) for r in range(pps)] * 2,
            out_specs=batch3((t, w)),
            scratch_shapes=[pltpu.VMEM((rows, w), F32), pltpu.VMEM((rows, 1), F32), pltpu.SMEM((1,), I32)],
        ),
        out_shape=jax.ShapeDtypeStruct((db, t, w), F32),
        compiler_params=_params(("parallel", "arbitrary")),
        name="sb_sample_tail",
    )(page_table, alive, q_bd, acc, carry, *([pk] * pps), *([pv] * pps))


def _dsa_sample_select_kernel(pt_ref, qi_ref, wi_ref, inew_ref, *rest, pps, t_dec, topk, past, cb):
    page_refs = rest[:pps]
    bias_ref, key_ref = rest[pps:]
    j = pl.program_id(1)
    qi = qi_ref[0]
    wi = wi_ref[0]
    nkeys = past + cb
    nblk = nkeys // cb

    def scores(kt):
        s = jnp.maximum(jnp.dot(qi, kt.astype(BF16), preferred_element_type=F32), 0.0) * wi
        sc = jnp.zeros((t_dec, kt.shape[1]), F32)
        for h in range(N_IDX_HEADS):
            sc = sc + s[h * t_dec:(h + 1) * t_dec, :]
        return _sortable_key(sc)

    c0 = pl.multiple_of(j * (pps * PAGE), pps * PAGE)
    key_ref[:, pl.ds(c0, pps * PAGE)] = scores(jnp.concatenate([page_refs[r][0, 0] for r in range(pps)], axis=1))

    @pl.when(j == pl.num_programs(1) - 1)
    def _():
        t_row = lax.broadcasted_iota(I32, (t_dec, 1), 0)
        lane = lax.broadcasted_iota(I32, (1, cb), 1)
        new = jnp.where(lax.broadcasted_iota(I32, (1, PAGE), 1) <= t_row, scores(inew_ref[0]), INT_MIN)
        key_ref[:, past:past + cb] = jnp.concatenate([new, jnp.full((t_dec, cb - PAGE), INT_MIN, I32)], axis=1)

        def count(pred):
            def body(c, acc):
                c0 = pl.multiple_of(c * cb, cb)
                hit = jnp.where(pred(key_ref[:, pl.ds(c0, cb)], c0 + lane), 1, 0)
                return acc + hit
            acc = lax.fori_loop(0, nblk, body, jnp.zeros((t_dec, cb), I32))
            return acc.sum(axis=1, keepdims=True)

        kq = jnp.minimum(past + t_row + 1, topk)
        none = jnp.zeros((t_dec, 1), I32)
        thr, found, sep, pending = _radix_select(
            lambda cand: count(lambda k, p: k >= cand), kq, 32, RADIX_GROUP, (t_dec, 1), none, none, lambda c: c)
        thr = jnp.where(found > 0, sep, thr)
        cut = _tie_cut(count, thr, kq, found, pending, (t_dec, 1), max(1, (nkeys - 1).bit_length()))

        def bias_body(c, carry):
            c0 = pl.multiple_of(c * cb, cb)
            k = key_ref[:, pl.ds(c0, cb)]
            sel = (k > thr) | ((k == thr) & (c0 + lane <= cut))
            bias_ref[0, :, pl.ds(c0, cb)] = jnp.where(sel, 0.0, MASK_BIAS).astype(F32)
            return carry

        lax.fori_loop(0, nblk, bias_body, 0)


def _dsa_sample_attend_kernel(pt_ref, q_ref, bias_ref, bnew_ref, knew_ref, vnew_ref, *rest, pps, t_dec):
    k_refs = rest[:pps]
    v_refs = rest[pps:2 * pps]
    o_ref, m_ref, l_ref, acc_ref = rest[2 * pps:]
    j = pl.program_id(1)
    kvw = N_KV_A * HEAD_DIM

    @pl.when(j == 0)
    def _():
        m_ref[...] = jnp.full_like(m_ref, M_INIT)
        l_ref[...] = jnp.zeros_like(l_ref)
        acc_ref[...] = jnp.zeros_like(acc_ref)

    def attend(kt, vt, bias):
        bias4 = jnp.concatenate([bias] * GROUP_A, axis=0)
        for c in range(N_KV_A):
            lg = jnp.dot(q_ref[0, c], kt, preferred_element_type=F32) + bias4
            m = m_ref[c]
            m_new = jnp.maximum(m, lg.max(axis=1, keepdims=True))
            alpha = jnp.exp2(m - m_new)
            p = jnp.exp2(lg - m_new)
            l_ref[c] = alpha * l_ref[c] + p.sum(axis=1, keepdims=True)
            acc_ref[c] = alpha * acc_ref[c] + _dot_nt(p.astype(BF16), vt)
            m_ref[c] = m_new

    kt = jnp.concatenate([k_refs[r][0, 0].reshape(kvw, PAGE) for r in range(pps)], axis=1).astype(BF16)
    vt = jnp.concatenate([v_refs[r][0, 0].reshape(kvw, PAGE) for r in range(pps)], axis=1).astype(BF16)
    attend(kt, vt, bias_ref[0])

    @pl.when(j == pl.num_programs(1) - 1)
    def _():
        attend(knew_ref[0].astype(BF16), vnew_ref[0].astype(BF16), bnew_ref[0])
        for c in range(N_KV_A):
            o_ref[0, c] = acc_ref[c] / l_ref[c]


def dsa_sample(qa, ka, va, qi, ki, wi, pool_k, pool_v, pool_ik, layer, page_table, topk, pps=16, cb=512):
    db, t = qa.shape[:2]
    n_pages = page_table.shape[1]
    past = n_pages * PAGE
    while n_pages % pps:
        pps //= 2
    cb = min(cb, pps * PAGE)
    nsteps = n_pages // pps
    kvw = N_KV_A * HEAD_DIM
    pk, pv, pik = _keys_on_lanes(pool_k), _keys_on_lanes(pool_v), _keys_on_lanes(pool_ik)

    def page_specs(block):
        zeros = (0,) * (len(block) - 2)
        return [pl.BlockSpec(block, lambda b, j, pt, r=r: (layer, pt[b, j * pps + r]) + zeros) for r in range(pps)]

    hrows = N_IDX_HEADS * t
    qi_r = qi.transpose(0, 2, 1, 3).reshape(db, hrows, IDX_DIM).astype(BF16)
    wi_r = wi.transpose(0, 2, 1).reshape(db, hrows, 1)
    nkeys = past + cb
    bias = pl.pallas_call(
        functools.partial(_dsa_sample_select_kernel, pps=pps, t_dec=t, topk=topk, past=past, cb=cb),
        grid_spec=pltpu.PrefetchScalarGridSpec(
            num_scalar_prefetch=1,
            grid=(db, nsteps),
            in_specs=[pl.BlockSpec((1, hrows, IDX_DIM), lambda b, j, pt: (b, 0, 0)),
                      pl.BlockSpec((1, hrows, 1), lambda b, j, pt: (b, 0, 0)),
                      pl.BlockSpec((1, IDX_DIM, PAGE), lambda b, j, pt: (b, 0, 0))]
                     + page_specs((1, 1, IDX_DIM, PAGE)),
            out_specs=pl.BlockSpec((1, t, nkeys), lambda b, j, pt: (b, 0, 0)),
            scratch_shapes=[pltpu.VMEM((t, nkeys), I32)],
        ),
        out_shape=jax.ShapeDtypeStruct((db, t, nkeys), F32),
        compiler_params=_params(("parallel", "arbitrary")),
        name="dsa_sample_select",
    )(page_table, qi_r, wi_r, _new_keys_on_lanes(ki), *([pik] * pps))

    q5 = (qa * (HEAD_DIM ** -0.5 * LOG2E)).reshape(db, t, N_KV_A, GROUP_A, HEAD_DIM).transpose(0, 2, 3, 1, 4)
    q5 = q5.reshape(db, N_KV_A, GROUP_A * t, HEAD_DIM)
    eye = jnp.eye(N_KV_A, dtype=F32)
    q_pad = (q5[:, :, :, None, :] * eye[None, :, None, :, None]).reshape(db, N_KV_A, GROUP_A * t, kvw).astype(BF16)
    rows = GROUP_A * t
    kv_block = (1, 1, N_KV_A, HEAD_DIM, PAGE)
    out = pl.pallas_call(
        functools.partial(_dsa_sample_attend_kernel, pps=pps, t_dec=t),
        grid_spec=pltpu.PrefetchScalarGridSpec(
            num_scalar_prefetch=1,
            grid=(db, nsteps),
            in_specs=[pl.BlockSpec((1, N_KV_A, rows, kvw), lambda b, j, pt: (b, 0, 0, 0)),
                      pl.BlockSpec((1, t, pps * PAGE), lambda b, j, pt: (b, 0, j)),
                      pl.BlockSpec((1, t, PAGE), lambda b, j, pt: (b, 0, past // PAGE)),
                      pl.BlockSpec((1, kvw, PAGE), lambda b, j, pt: (b, 0, 0)),
                      pl.BlockSpec((1, kvw, PAGE), lambda b, j, pt: (b, 0, 0))]
                     + page_specs(kv_block) + page_specs(kv_block),
            out_specs=pl.BlockSpec((1, N_KV_A, rows, kvw), lambda b, j, pt: (b, 0, 0, 0)),
            scratch_shapes=[pltpu.VMEM((N_KV_A, rows, 1), F32), pltpu.VMEM((N_KV_A, rows, 1), F32),
                            pltpu.VMEM((N_KV_A, rows, kvw), F32)],
        ),
        out_shape=jax.ShapeDtypeStruct((db, N_KV_A, rows, kvw), F32),
        compiler_params=_params(("parallel", "arbitrary")),
        name="dsa_sample_attend",
    )(page_table, q_pad, bias, bias, _new_keys_on_lanes(ka.reshape(db, t, kvw)), _new_keys_on_lanes(va.reshape(db, t, kvw)),
      *([pk] * pps), *([pv] * pps))
    o = out.reshape(db, N_KV_A, GROUP_A, t, N_KV_A, HEAD_DIM)
    o = jnp.stack([o[:, c, :, :, c, :] for c in range(N_KV_A)], axis=1)
    return o.transpose(0, 3, 1, 2, 4).reshape(db, t, N_HEADS_A * HEAD_DIM)


def _mem_kernel(x_ref, wq_ref, mk_ref, mv_ref, wo_ref, o_ref):
    x = x_ref[...].reshape(x_ref.shape[-2:]).astype(BF16)
    mk = mk_ref[...].reshape(mk_ref.shape[-2:]).astype(BF16)
    mv = mv_ref[...].reshape(mv_ref.shape[-2:]).astype(BF16)
    q = jnp.dot(x, wq_ref[...], preferred_element_type=F32).astype(BF16)
    outs = []
    for h in range(N_MEM_HEADS):
        sl = slice(h * MEM_HEAD_DIM, (h + 1) * MEM_HEAD_DIM)
        lg = _dot_nt(q[:, sl], mk[:, sl]) * (MEM_HEAD_DIM ** -0.5)
        p = jnp.exp(lg - lg.max(axis=1, keepdims=True))
        p = p / p.sum(axis=1, keepdims=True)
        outs.append(jnp.dot(p.astype(BF16), mv[:, sl], preferred_element_type=F32))
    o = jnp.concatenate(outs, axis=1).astype(BF16)
    o_ref[...] = jnp.dot(o, wo_ref[...], preferred_element_type=F32).reshape(o_ref.shape)


def mem_attend(x, mk, mv, wq, wo, tm):
    b, t, d = x.shape
    n_mem, w = mk.shape[1:]
    return pl.pallas_call(
        _mem_kernel,
        grid=(b, t // tm),
        in_specs=[pl.BlockSpec((1, tm, d), lambda bi, i: (bi, i, 0)),
                  pl.BlockSpec(wq.shape, lambda bi, i: (0, 0)),
                  pl.BlockSpec((1, n_mem, w), lambda bi, i: (bi, 0, 0)),
                  pl.BlockSpec((1, n_mem, w), lambda bi, i: (bi, 0, 0)),
                  pl.BlockSpec(wo.shape, lambda bi, i: (0, 0))],
        out_specs=pl.BlockSpec((1, tm, d), lambda bi, i: (bi, i, 0)),
        out_shape=jax.ShapeDtypeStruct((b, t, d), F32),
        compiler_params=_params(("parallel", "parallel")),
        name="mem_attend",
    )(x, wq, mk, mv, wo)


def _split3_nt(w, x):
    wh = w.astype(BF16)
    wl = (w - wh.astype(F32)).astype(BF16)
    xh = x.astype(BF16)
    xl = (x - xh.astype(F32)).astype(BF16)
    return _dot_nt(wh, xh) + (_dot_nt(wh, xl) + _dot_nt(wl, xh))


def _router_kernel(x_ref, w_ref, b_ref, o_ref):
    logits = _split3_nt(w_ref[...], x_ref[...]) + b_ref[...]
    rows = [logits[e:e + 1, :] for e in range(N_EXPERTS)]
    mx = functools.reduce(jnp.maximum, rows)
    ex = [jnp.exp(r - mx) for r in rows]
    tot = functools.reduce(lambda a, b: a + b, ex)
    p = [e / tot for e in ex]
    gscore = []
    for g in range(N_GROUPS):
        a, b, c, d = p[4 * g:4 * g + 4]
        h1, l1, h2, l2 = jnp.maximum(a, b), jnp.minimum(a, b), jnp.maximum(c, d), jnp.minimum(c, d)
        gscore.append(jnp.maximum(h1, h2) + jnp.maximum(jnp.minimum(h1, h2), jnp.maximum(l1, l2)))
    best = gscore[0]
    group = jnp.zeros_like(best, dtype=I32)
    for g in range(1, N_GROUPS):
        better = gscore[g] > best
        best = jnp.where(better, gscore[g], best)
        group = jnp.where(better, g, group)
    sel = []
    for e in range(N_EXPERTS):
        g = e // EXPERTS_PER_GROUP
        rank = jnp.zeros_like(group)
        for o in range(g * EXPERTS_PER_GROUP, (g + 1) * EXPERTS_PER_GROUP):
            if o != e:
                ahead = (p[o] >= p[e]) if o < e else (p[o] > p[e])
                rank = rank + jnp.where(ahead, 1, 0)
        sel.append((group == g) & (rank < 2))
    top_sum = functools.reduce(lambda a, b: a + b, [jnp.where(s, pe, 0.0) for s, pe in zip(sel, p)])
    o_ref[...] = jnp.concatenate([jnp.where(s, pe / top_sum, 0.0) for s, pe in zip(sel, p)], axis=0)


def router(x, w_router_t, b_router, tm):
    n, d = x.shape
    return pl.pallas_call(
        _router_kernel,
        grid=(n // tm,),
        in_specs=[pl.BlockSpec((tm, d), lambda i: (i, 0)),
                  pl.BlockSpec((N_EXPERTS, d), lambda i: (0, 0)),
                  pl.BlockSpec((N_EXPERTS, 1), lambda i: (0, 0))],
        out_specs=pl.BlockSpec((N_EXPERTS, tm), lambda i: (0, i)),
        out_shape=jax.ShapeDtypeStruct((N_EXPERTS, n), F32),
        compiler_params=_params(("parallel",)),
        name="router",
    )(x, w_router_t, b_router.reshape(N_EXPERTS, 1))


def _moe_kernel(x_ref, wg_ref, wu_ref, wd_ref, comb_ref, o_ref):
    e = pl.program_id(1)
    x = x_ref[...].astype(BF16)
    g = jnp.dot(x, wg_ref[0], preferred_element_type=F32)
    u = jnp.dot(x, wu_ref[0], preferred_element_type=F32)
    h = (g * jax.nn.sigmoid(g)) * u
    y = jnp.dot(h.astype(BF16), wd_ref[0], preferred_element_type=F32)
    lane = lax.broadcasted_iota(I32, (1, N_EXPERTS), 1)
    ce = jnp.sum(jnp.where(lane == e, comb_ref[...], 0.0), axis=1, keepdims=True)

    @pl.when(e == 0)
    def _():
        o_ref[...] = ce * y

    @pl.when(e > 0)
    def _():
        o_ref[...] += ce * y


def moe_experts(x, comb, wg, wu, wd, tm):
    n, d = x.shape
    de = wg.shape[2]
    return pl.pallas_call(
        _moe_kernel,
        grid=(n // tm, N_EXPERTS),
        in_specs=[pl.BlockSpec((tm, d), lambda i, e: (i, 0)),
                  pl.BlockSpec((1, d, de), lambda i, e: (e, 0, 0)),
                  pl.BlockSpec((1, d, de), lambda i, e: (e, 0, 0)),
                  pl.BlockSpec((1, de, d), lambda i, e: (e, 0, 0)),
                  pl.BlockSpec((tm, N_EXPERTS), lambda i, e: (i, 0))],
        out_specs=pl.BlockSpec((tm, d), lambda i, e: (i, 0)),
        out_shape=jax.ShapeDtypeStruct((n, d), F32),
        compiler_params=_params(("parallel", "arbitrary")),
        name="moe_experts",
    )(x, wg, wu, wd, comb)


def _layer_norm(x, g, b):
    mu = x.mean(-1, keepdims=True)
    var = jnp.square(x - mu).mean(-1, keepdims=True)
    return (x - mu) * lax.rsqrt(var + LN_EPS) * g + b


def _rope(x, pos):
    rot = x.shape[-1] // 4
    half = rot // 2
    inv_freq = jnp.power(ROPE_THETA, -jnp.arange(half, dtype=F32) * 2.0 / rot)
    ang = pos.astype(F32)[:, None] * inv_freq[None, :]
    cos = jnp.cos(ang)[:, None, :]
    sin = jnp.sin(ang)[:, None, :]
    x1, x2 = x[..., :half], x[..., half:rot]
    return jnp.concatenate([x1 * cos - x2 * sin, x2 * cos + x1 * sin, x[..., rot:]], axis=-1)


def _pack_w_in(w_in_l):
    cols = [w_in_l[:, _SRC[n][0]:_SRC[n][0] + _SRC[n][1]] for n in _ORDER]
    packed = jnp.concatenate(cols, axis=1)
    return jnp.pad(packed, ((0, 0), (0, W_IN_PACKED - packed.shape[1]))).astype(BF16)


def _take(proj, name):
    o, n = _DST[name]
    return proj[..., o:o + n]


def _mixer_inputs(proj, pos):
    lead = proj.shape[:-1]
    hd = lambda name, h: _take(proj, name).reshape(lead + (h, HEAD_DIM))
    rp = lambda a: _rope(a, pos)
    qa, ka, va = rp(hd("qa", N_HEADS_A)), rp(hd("ka", N_KV_A)), hd("va", N_KV_A)
    qi = rp(hd("qi", N_IDX_HEADS))
    ki = rp(_take(proj, "ki")[..., None, :])[..., 0, :]
    wi = _take(proj, "wi")
    return qa, ka, va, qi, ki, wi, hd("kb", N_HEADS_B), hd("vb", N_HEADS_B)


def _tail(x2, mix, mem_fn, lw, tm):
    alpha = lw["alpha"]
    x2 = _layer_norm(alpha * x2 + mix, lw["ln1_g"], lw["ln1_b"])
    x2 = _layer_norm(alpha * x2 + mem_fn(x2), lw["ln2_g"], lw["ln2_b"])
    comb = router(x2, lw["w_router_t"], lw["b_router"], tm).T
    y = moe_experts(x2, comb, lw["w_gate"], lw["w_up"], lw["w_down"], tm)
    return _layer_norm(alpha * x2 + y, lw["ln3_g"], lw["ln3_b"])


def kernel(x_prompt, x_sample, cache_a_k, cache_a_v, cache_idx_k, cache_b_k, cache_b_v, cache_mem_k, cache_mem_v,
           page_table, mem_prompt, w_in, w_pa, w_pb, w_o, ln1_g, ln1_b, w_cq, w_ck, w_cv, w_co, ln2_g, ln2_b,
           w_router, b_router, w_gate, w_up, w_down, ln3_g, ln3_b):
    depth = w_in.shape[0]
    bp, seq, d = x_prompt.shape
    db, t_dec, _ = x_sample.shape
    n_mem = mem_prompt.shape[1]
    past = page_table.shape[1] * PAGE
    k_prompt = min(MAX_SELECT, seq // 4)
    k_sample = min(MAX_SELECT, (past + t_dec) // 4)
    pos_p = jnp.arange(seq)
    pos_s = past + jnp.arange(t_dec)
    alpha = (2 * depth) ** 0.25
    tm_p = min(1024, seq)
    n_s = db * t_dec
    wm = N_MEM_HEADS * MEM_HEAD_DIM

    xp = x_prompt.reshape(bp * seq, d)
    xs = x_sample.reshape(n_s, d)
    outs = {k: [] for k in ("p_ak", "p_av", "p_ik", "p_bk", "p_bv", "p_mk", "p_mv", "s_ak", "s_av", "s_ik", "s_bk", "s_bv")}
    mem_bf = mem_prompt.reshape(bp * n_mem, d).astype(BF16)
    for l in range(depth):
        lw = dict(alpha=alpha, ln1_g=ln1_g[l], ln1_b=ln1_b[l], ln2_g=ln2_g[l], ln2_b=ln2_b[l],
                  ln3_g=ln3_g[l], ln3_b=ln3_b[l], w_router_t=w_router.T, b_router=b_router,
                  w_gate=w_gate[l].astype(BF16), w_up=w_up[l].astype(BF16), w_down=w_down[l].astype(BF16))
        w_in_p = _pack_w_in(w_in[l])
        wpa, wpb, wo = w_pa[l].astype(BF16), w_pb[l].astype(BF16), w_o[l].astype(BF16)
        wcq, wco = w_cq[l].astype(BF16), w_co[l].astype(BF16)
        wckv = jnp.concatenate([w_ck[l], w_cv[l]], axis=1).astype(BF16)

        proj = matmul(xp.astype(BF16), w_in_p, tm_p, 1024)
        proj_b = proj.reshape(bp, seq, W_IN_PACKED)
        qa, ka, va, qi, ki, wi, kb, vb = _mixer_inputs(proj_b, pos_p)
        ya = jnp.concatenate([dsa_prompt(qa[b], ka[b], va[b], qi[b], ki[b], wi[b], k_prompt) for b in range(bp)], axis=0)
        qb_s = (_take(proj_b, "qb") * (HEAD_DIM ** -0.5)).astype(BF16)
        kb_s = _take(proj_b, "kb").astype(BF16)
        vb_s = _take(proj_b, "vb").astype(BF16)
        yb = jnp.concatenate([sb_prompt(qb_s[b], kb_s[b], vb_s[b]) for b in range(bp)], axis=0)
        mix = gated_merge(ya, yb, proj, wpa, wpb, wo, tm_p)
        mkv = matmul(mem_bf, wckv, bp * n_mem, wm)
        mk, mv = mkv[:, :wm].reshape(bp, n_mem, wm), mkv[:, wm:].reshape(bp, n_mem, wm)
        mem_fn = lambda x2: mem_attend(x2.reshape(bp, seq, d), mk, mv, wcq, wco, tm_p).reshape(bp * seq, d)
        xp = _tail(xp, mix, mem_fn, lw, tm_p)
        outs["p_ak"].append(ka); outs["p_av"].append(va); outs["p_ik"].append(ki)
        outs["p_bk"].append(kb); outs["p_bv"].append(vb)
        outs["p_mk"].append(mk.reshape(bp, n_mem, N_MEM_HEADS, MEM_HEAD_DIM))
        outs["p_mv"].append(mv.reshape(bp, n_mem, N_MEM_HEADS, MEM_HEAD_DIM))

        proj = matmul(xs.astype(BF16), w_in_p, n_s, 1024)
        proj_b = proj.reshape(db, t_dec, W_IN_PACKED)
        qa, ka, va, qi, ki, wi, kb, vb = _mixer_inputs(proj_b, pos_s)
        ya = dsa_sample(qa, ka, va, qi, ki, wi, cache_a_k, cache_a_v, cache_idx_k, l, page_table, k_sample)
        yb = sb_sample(_take(proj_b, "qb"), _take(proj_b, "kb"), _take(proj_b, "vb"), cache_b_k, cache_b_v, l, page_table)
        mix = gated_merge(ya.reshape(n_s, -1), yb.reshape(n_s, -1), proj, wpa, wpb, wo, n_s)
        cmk = cache_mem_k[l].reshape(db, n_mem, wm)
        cmv = cache_mem_v[l].reshape(db, n_mem, wm)
        mem_fn = lambda x2: mem_attend(x2.reshape(db, t_dec, d), cmk, cmv, wcq, wco, t_dec).reshape(n_s, d)
        xs = _tail(xs, mix, mem_fn, lw, n_s)
        outs["s_ak"].append(ka); outs["s_av"].append(va); outs["s_ik"].append(ki)
        outs["s_bk"].append(kb); outs["s_bv"].append(vb)

    st = lambda k: jnp.stack(outs[k])
    return (xp.reshape(bp, seq, d), xs.reshape(db, t_dec, d),
            st("p_ak"), st("p_av"), st("p_ik"), st("p_bk"), st("p_bv"), st("p_mk"), st("p_mv"),
            st("s_ak"), st("s_av"), st("s_ik"), st("s_bk"), st("s_bv"))
```

```python
import functools

import jax
import jax.numpy as jnp
from jax import lax
from jax.experimental import pallas as pl
from jax.experimental.pallas import tpu as pltpu

F32 = jnp.float32
BF16 = jnp.bfloat16
I32 = jnp.int32

HEAD_DIM = 64
N_HEADS_A = 8
N_KV_A = 2
GROUP_A = N_HEADS_A // N_KV_A
N_IDX_HEADS = 8
IDX_DIM = 64
N_HEADS_B = 8
MAX_SELECT = 256
ROPE_THETA = 500000.0
PAGE = 128
N_MEM_HEADS = 4
MEM_HEAD_DIM = 128
N_EXPERTS = 16
N_GROUPS = 4
EXPERTS_PER_GROUP = 4
LN_EPS = 1e-5

LANES = 128
SUBLANES = 8
VMEM_LIMIT = 56 * 2**20
N_ACC = 4

INT_MIN = -(2**31)
INT_MAX = 2**31 - 1
M_INIT = -1e30
MASK_BIAS = -2e30
NO_LIMIT = 1e9
LOG2E = 1.4426950408889634
RADIX_GROUP = 4
LAZY_SUM_LIMIT = 2.0 ** 30
SB_DEAD = -104.0

_SRC = dict(qa=(0, 512), ka=(512, 128), va=(640, 128), qi=(768, 512), ki=(1280, 64), wi=(1344, 8),
            qb=(1352, 512), kb=(1864, 512), vb=(2376, 512), ga=(2888, 1024), gb=(3912, 1024))
_ORDER = ("ga", "gb", "qa", "qi", "qb", "kb", "vb", "ka", "va", "ki", "wi")
_DST = {}
_off = 0
for _n in _ORDER:
    _DST[_n] = (_off, _SRC[_n][1])
    _off += _SRC[_n][1]
W_IN_PACKED = 5120


def _params(sem):
    return pltpu.CompilerParams(dimension_semantics=sem, vmem_limit_bytes=VMEM_LIMIT)


def _dot_nt(a, b):
    return lax.dot_general(a, b, (((1,), (1,)), ((), ())), preferred_element_type=F32)


def _mm_kernel(x_ref, w_ref, o_ref):
    o_ref[...] = jnp.dot(x_ref[...].astype(BF16), w_ref[...], preferred_element_type=F32).astype(o_ref.dtype)


def matmul(x, w, tm, tn, out_dtype=F32):
    m, k = x.shape
    n = w.shape[1]
    return pl.pallas_call(
        _mm_kernel,
        grid=(m // tm, n // tn),
        in_specs=[pl.BlockSpec((tm, k), lambda i, j: (i, 0)), pl.BlockSpec((k, tn), lambda i, j: (0, j))],
        out_specs=pl.BlockSpec((tm, tn), lambda i, j: (i, j)),
        out_shape=jax.ShapeDtypeStruct((m, n), out_dtype),
        compiler_params=_params(("parallel", "arbitrary")),
        name="matmul",
    )(x, w)


def _residual_norm(x, y, alpha, g, b):
    v = alpha * x + y
    mu = jnp.mean(v, axis=-1, keepdims=True)
    dev = v - mu
    var = jnp.mean(dev * dev, axis=-1, keepdims=True)
    return dev * lax.rsqrt(var + LN_EPS) * g + b


def _merge_kernel(x_ref, ya_ref, yb_ref, ga_ref, gb_ref, wpa_ref, wpb_ref, wo_ref, g_ref, b_ref, o_ref, *, alpha):
    a = jnp.dot(ya_ref[...].astype(BF16), wpa_ref[...], preferred_element_type=F32)
    b = jnp.dot(yb_ref[...].astype(BF16), wpb_ref[...], preferred_element_type=F32)
    merged = jax.nn.sigmoid(ga_ref[...]) * a + jax.nn.sigmoid(gb_ref[...]) * b
    mix = jnp.dot(merged.astype(BF16), wo_ref[...], preferred_element_type=F32)
    o_ref[...] = _residual_norm(x_ref[...], mix, alpha, g_ref[...], b_ref[...])


def gated_merge(x, ya, yb, proj, wpa, wpb, wo, ln_g, ln_b, alpha, tm):
    m = ya.shape[0]
    d = wo.shape[1]
    full = lambda a: pl.BlockSpec(a.shape, lambda i: (0, 0))
    row = pl.BlockSpec((1, d), lambda i: (0, 0))
    return pl.pallas_call(
        functools.partial(_merge_kernel, alpha=alpha),
        grid=(m // tm,),
        in_specs=[pl.BlockSpec((tm, d), lambda i: (i, 0)),
                  pl.BlockSpec((tm, ya.shape[1]), lambda i: (i, 0)),
                  pl.BlockSpec((tm, yb.shape[1]), lambda i: (i, 0)),
                  pl.BlockSpec((tm, d), lambda i: (i, _DST["ga"][0] // d)),
                  pl.BlockSpec((tm, d), lambda i: (i, _DST["gb"][0] // d)),
                  full(wpa), full(wpb), full(wo), row, row],
        out_specs=pl.BlockSpec((tm, d), lambda i: (i, 0)),
        out_shape=jax.ShapeDtypeStruct((m, d), F32),
        compiler_params=_params(("parallel",)),
        name="gated_merge",
    )(x, ya, yb, proj, proj, wpa, wpb, wo, ln_g.reshape(1, d), ln_b.reshape(1, d))


def _sortable_key(score):
    bits = lax.bitcast_convert_type(score, I32)
    key = jnp.where(bits < 0, bits ^ INT_MAX, bits)
    return jnp.where(key == -1, 0, key)


def _radix_select(count_ge, kq, width, group, shape, found, sep, full_key):
    low = -(1 << (width - 1))

    def probe(cand, prefix, found, sep):
        cnt = count_ge(cand)
        sep = jnp.where((cnt == kq) & (found == 0), full_key(cand), sep)
        return jnp.where(cnt >= kq, cand, prefix), jnp.where(cnt == kq, 1, found), sep

    prefix, found, sep = probe(jnp.zeros(shape, I32), jnp.full(shape, low, I32), found, sep)
    n_groups = -(-(width - 1) // group)

    def group_cond(st):
        return (st[0] < n_groups) & (st[4] > 0)

    def group_body(st):
        gi, prefix, found, sep, _ = st

        def bit_body(j, inner):
            prefix, found, sep = inner
            shift = width - 2 - (gi * group + j)
            bit = jnp.where(shift >= 0, jnp.left_shift(jnp.int32(1), jnp.maximum(shift, 0)), 0)
            return probe(prefix | bit, prefix, found, sep)

        prefix, found, sep = lax.fori_loop(0, group, bit_body, (prefix, found, sep))
        return gi + 1, prefix, found, sep, jnp.max(1 - found)

    _, prefix, found, sep, pending = lax.while_loop(
        group_cond, group_body, (jnp.int32(0), prefix, found, sep, jnp.max(1 - found)))
    return prefix, found, sep, pending


def _tie_cut(count, thr, kq, found, pending, shape, idx_bits):
    def tie_search():
        need = kq - count(lambda k, p: k > thr)

        def tie_body(bi, cut):
            cand = cut | jnp.left_shift(jnp.int32(1), idx_bits - 1 - bi)
            below = count(lambda k, p: (k == thr) & (p < cand))
            return jnp.where(below < need, cand, cut)

        cut = lax.fori_loop(0, idx_bits, tie_body, jnp.zeros(shape, I32))
        return jnp.where(found > 0, INT_MAX, cut)

    return lax.cond(pending > 0, tie_search, lambda: jnp.full(shape, INT_MAX, I32))


def _dsa_prompt_kernel(kidx_ref, qi_ref, wi_ref, k_ref, vt_ref, q_ref, o_ref, key_ref, *, tq, kc, sub, topk):
    i = pl.program_id(0)
    nchunk = ((i + 1) * tq + kc - 1) // kc
    qpos = i * tq + lax.broadcasted_iota(I32, (1, tq), 1)
    row_iota = lax.broadcasted_iota(I32, (kc, 1), 0)

    qi = qi_ref[0]
    wi = wi_ref[0]

    def idx_body(c, carry):
        for j in range(kc // sub):
            r0 = pl.multiple_of(c * kc + j * sub, sub)
            s = jnp.dot(kidx_ref[pl.ds(r0, sub), :], qi, preferred_element_type=F32)
            sc = jnp.zeros((sub, tq), F32)
            for h in range(N_IDX_HEADS):
                sc = sc + jnp.maximum(s[:, h * tq:(h + 1) * tq], 0.0) * wi[h:h + 1, :]
            key = jnp.where(r0 + row_iota[:sub] <= qpos, _sortable_key(sc), INT_MIN)
            key_ref[pl.ds(r0, sub), :] = key
        return carry

    lax.fori_loop(0, nchunk, idx_body, 0)

    def count(pred):
        def body(c, acc):
            r0 = pl.multiple_of(c * kc, kc)
            hit = jnp.where(pred(key_ref[pl.ds(r0, kc), :], r0 + row_iota), 1, 0)
            return acc + hit.reshape(kc // (N_ACC * SUBLANES), N_ACC * SUBLANES, tq).sum(axis=0)
        acc = lax.fori_loop(0, nchunk, body, jnp.zeros((N_ACC * SUBLANES, tq), I32))
        return acc.sum(axis=0, keepdims=True)

    kq = jnp.minimum(qpos + 1, topk)
    none = jnp.zeros((1, tq), I32)
    thr, found, sep, pending = _radix_select(
        lambda cand: count(lambda k, p: k >= cand), kq, 32, RADIX_GROUP, (1, tq), none, none, lambda c: c)
    thr = jnp.where(found > 0, sep, thr)
    n_gt = lax.cond(pending > 0, lambda: count(lambda k, p: k > thr), lambda: none)
    need = jnp.where(found > 0, NO_LIMIT, (kq - n_gt).astype(F32))
    earlier = jnp.where(lax.broadcasted_iota(I32, (sub, sub), 1) < lax.broadcasted_iota(I32, (sub, sub), 0), 1.0, 0.0).astype(BF16)

    def bias_body(c, ties_before):
        for j in range(kc // sub):
            r0 = pl.multiple_of(c * kc + j * sub, sub)
            k = key_ref[pl.ds(r0, sub), :]
            tie = jnp.where(k == thr, 1.0, 0.0)
            before = jnp.dot(earlier, tie.astype(BF16), preferred_element_type=F32) + ties_before
            sel = (k > thr) | ((k == thr) & (before < need))
            key_ref[pl.ds(r0, sub), :] = lax.bitcast_convert_type(jnp.where(sel, 0.0, MASK_BIAS).astype(F32), I32)
            ties_before = ties_before + tie.sum(axis=0, keepdims=True)
        return ties_before

    lax.fori_loop(0, nchunk, bias_body, jnp.zeros((1, tq), F32))

    nq = GROUP_A * tq

    def att_body(ch, carry):
        def logits(c, r0, n):
            bias = lax.bitcast_convert_type(key_ref[pl.ds(r0, n), :], F32)
            bias = jnp.concatenate([bias] * GROUP_A, axis=1)
            return jnp.dot(k_ref[pl.ds(r0, n), :], q_ref[0, c], preferred_element_type=F32) + bias

        def pv(c, p, r0, n):
            vt = vt_ref[c * HEAD_DIM:(c + 1) * HEAD_DIM, pl.ds(r0, n)]
            return jnp.dot(vt, p.astype(BF16), preferred_element_type=F32)

        r0 = pl.multiple_of(ch * kc, kc)
        lazy, overrun = [], jnp.float32(0.0)
        lgs = [logits(c, r0, kc) for c in range(N_KV_A)]
        for c in range(N_KV_A):
            m, l, acc = carry[c]
            p = jnp.exp2(lgs[c] - m)
            l = l + p.sum(axis=0, keepdims=True)
            lazy.append((m, l, acc + pv(c, p, r0, kc)))
            overrun = jnp.maximum(overrun, jnp.max(jnp.where(l < LAZY_SUM_LIMIT, 0.0, 1.0)))

        def keep_lazy():
            return tuple(lazy)

        def redo_with_chunk_max():
            out = []
            for c in range(N_KV_A):
                m, l, acc = carry[c]
                lg = logits(c, r0, kc)
                m_new = jnp.maximum(m, lg.max(axis=0, keepdims=True))
                alpha = jnp.exp2(m - m_new)
                p = jnp.exp2(lg - m_new)
                out.append((m_new, alpha * l + p.sum(axis=0, keepdims=True), alpha * acc + pv(c, p, r0, kc)))
            return tuple(out)

        return lax.cond(overrun > 0.0, redo_with_chunk_max, keep_lazy)

    init = (jnp.full((1, nq), M_INIT, F32), jnp.zeros((1, nq), F32), jnp.zeros((HEAD_DIM, nq), F32))
    res = lax.fori_loop(0, nchunk, att_body, (init,) * N_KV_A)
    for c in range(N_KV_A):
        o_ref[0, c] = res[c][2] / res[c][1]


def dsa_prompt(qa, ka, va, qi, ki, wi, topk, tq=128, kc=1024, sub=128):
    s = qa.shape[0]
    kc = min(kc, s)
    nt = s // tq
    kidx = ki.astype(BF16)
    qi_t = qi.reshape(nt, tq, N_IDX_HEADS, IDX_DIM).transpose(0, 3, 2, 1).reshape(nt, IDX_DIM, N_IDX_HEADS * tq).astype(BF16)
    wi_t = wi.reshape(nt, tq, N_IDX_HEADS).transpose(0, 2, 1)
    k2 = ka.reshape(s, N_KV_A * HEAD_DIM).astype(BF16)
    v2t = va.reshape(s, N_KV_A * HEAD_DIM).T.astype(BF16)
    q5 = (qa * (HEAD_DIM ** -0.5 * LOG2E)).reshape(nt, tq, N_KV_A, GROUP_A, HEAD_DIM).transpose(0, 2, 4, 3, 1)
    q5 = q5.reshape(nt, N_KV_A, HEAD_DIM, GROUP_A * tq)
    eye = jnp.eye(N_KV_A, dtype=F32)
    q_pad = (eye[None, :, :, None, None] * q5[:, :, None, :, :]).reshape(nt, N_KV_A, N_KV_A * HEAD_DIM, GROUP_A * tq)
    q_pad = q_pad.astype(BF16)
    out = pl.pallas_call(
        functools.partial(_dsa_prompt_kernel, tq=tq, kc=kc, sub=min(sub, kc), topk=topk),
        grid=(nt,),
        in_specs=[pl.BlockSpec((s, IDX_DIM), lambda i: (0, 0)),
                  pl.BlockSpec((1, IDX_DIM, N_IDX_HEADS * tq), lambda i: (i, 0, 0)),
                  pl.BlockSpec((1, N_IDX_HEADS, tq), lambda i: (i, 0, 0)),
                  pl.BlockSpec((s, N_KV_A * HEAD_DIM), lambda i: (0, 0)),
                  pl.BlockSpec((N_KV_A * HEAD_DIM, s), lambda i: (0, 0)),
                  pl.BlockSpec((1, N_KV_A, N_KV_A * HEAD_DIM, GROUP_A * tq), lambda i: (i, 0, 0, 0))],
        out_specs=pl.BlockSpec((1, N_KV_A, HEAD_DIM, GROUP_A * tq), lambda i: (i, 0, 0, 0)),
        out_shape=jax.ShapeDtypeStruct((nt, N_KV_A, HEAD_DIM, GROUP_A * tq), F32),
        scratch_shapes=[pltpu.VMEM((s, tq), I32)],
        compiler_params=_params(("parallel",)),
        name="dsa_prompt",
    )(kidx, qi_t, wi_t, k2, v2t, q_pad)
    out = out.reshape(nt, N_KV_A, HEAD_DIM, GROUP_A, tq).transpose(0, 4, 1, 3, 2)
    return out.reshape(s, N_HEADS_A * HEAD_DIM)


def _tri(n):
    return jnp.where(lax.broadcasted_iota(I32, (n, n), 0) > lax.broadcasted_iota(I32, (n, n), 1), 1.0, 0.0).astype(BF16)


def _sb_blocks(zs, masks, vs, carries, tri, v_keys_on_lanes=False):
    n = len(zs)
    log_beta, log_keep, parts = [], [], []
    for z, mask in zip(zs, masks):
        soft = jnp.log1p(jnp.exp(-jnp.abs(z)))
        log_beta.append(jnp.minimum(z, 0.0) - soft)
        lk = -jnp.maximum(z, 0.0) - soft
        if mask is not None:
            lk = jnp.where(mask, lk, 0.0)
        log_keep.append(lk)
        hi = lk.astype(BF16)
        r1 = lk - hi.astype(F32)
        mid = r1.astype(BF16)
        parts.append((hi, mid, (r1 - mid.astype(F32)).astype(BF16)))
    after = [(jnp.dot(hi, tri, preferred_element_type=F32) + jnp.dot(mid, tri, preferred_element_type=F32)
              + jnp.dot(lo, tri, preferred_element_type=F32)) for hi, mid, lo in parts]
    weights = []
    for i in range(n):
        a = jnp.exp(log_beta[i] + (after[i] + carries[i]))
        if masks[i] is not None:
            a = jnp.where(masks[i], a, 0.0)
        weights.append(a.astype(BF16))
    outs = [_dot_nt(a, v) if v_keys_on_lanes else jnp.dot(a, v, preferred_element_type=F32)
            for a, v in zip(weights, vs)]
    return outs, [c + jnp.sum(lk, axis=1, keepdims=True) for c, lk in zip(carries, log_keep)]


def _sb_block(z, mask, v, carry, tri, v_keys_on_lanes=False):
    outs, carries = _sb_blocks([z], [mask], [v], [carry], tri, v_keys_on_lanes)
    return outs[0], carries[0]


def _sb_prompt_kernel(q_ref, k_ref, v_ref, o_ref, *, tq, nsub):
    i = pl.program_id(1)
    lane = lax.broadcasted_iota(I32, (1, 2 * HEAD_DIM), 1)
    row = lax.broadcasted_iota(I32, (tq, 1), 0)
    key_iota = lax.broadcasted_iota(I32, (1, tq), 1)
    tri = _tri(tq)
    q_all = q_ref[...]
    qh = [[jnp.where(lane // HEAD_DIM == hh, q_all[s * tq:(s + 1) * tq], jnp.zeros((tq, 2 * HEAD_DIM), q_all.dtype))
           for hh in range(2)] for s in range(nsub)]

    def cond(st):
        return st[1] > 0

    def body(st):
        d, _, carries, accs = st
        zs, masks, vs, older = [], [], [], []
        for s in range(nsub):
            tile = i * nsub + s
            c = tile - d
            r0 = pl.multiple_of(jnp.maximum(c, 0) * tq, tq)
            kblk = k_ref[pl.ds(r0, tq), :]
            mask = ((r0 + key_iota) < (tile * tq + row)) & (c >= 0)
            for hh in range(2):
                zs.append(_dot_nt(qh[s][hh], kblk))
                masks.append(mask)
                vs.append(v_ref[pl.ds(r0, tq), :])
                older.append(c >= 1)
        outs, new_c = _sb_blocks(zs, masks, vs, list(carries), tri)
        live = jnp.full((tq, 1), SB_DEAD, F32)
        for carry, more in zip(new_c, older):
            live = jnp.maximum(live, jnp.where(more, carry, SB_DEAD))
        alive = (jnp.max(live) > SB_DEAD).astype(I32)
        return d + 1, alive, tuple(new_c), tuple(a + o for a, o in zip(accs, outs))

    zc = jnp.zeros((tq, 1), F32)
    za = jnp.zeros((tq, 2 * HEAD_DIM), F32)
    accs = lax.while_loop(cond, body, (jnp.int32(0), jnp.int32(1), (zc,) * (2 * nsub), (za,) * (2 * nsub)))[3]
    for s in range(nsub):
        o_ref[s * tq:(s + 1) * tq, :] = jnp.where(lane < HEAD_DIM, accs[2 * s], accs[2 * s + 1])


def sb_prompt(q, k, v, tq=128, nsub=4):
    s, w = k.shape
    npair = w // (2 * HEAD_DIM)
    nsub = min(nsub, s // tq)
    rows = nsub * tq
    return pl.pallas_call(
        functools.partial(_sb_prompt_kernel, tq=tq, nsub=nsub),
        grid=(npair, s // rows),
        in_specs=[pl.BlockSpec((rows, 2 * HEAD_DIM), lambda h, i: (i, h)),
                  pl.BlockSpec((s, 2 * HEAD_DIM), lambda h, i: (0, h)),
                  pl.BlockSpec((s, 2 * HEAD_DIM), lambda h, i: (0, h))],
        out_specs=pl.BlockSpec((rows, 2 * HEAD_DIM), lambda h, i: (i, h)),
        out_shape=jax.ShapeDtypeStruct((s, w), F32),
        compiler_params=_params(("parallel", "parallel")),
        name="sb_prompt",
    )(q, k, v)


def _keys_on_lanes(pool):
    nd = pool.ndim
    return pool.transpose((0, 1) + tuple(range(3, nd)) + (2,))


def _new_keys_on_lanes(a):
    a = a.transpose(0, 2, 1)
    return jnp.pad(a, ((0, 0), (0, 0), (0, PAGE - a.shape[2])))


def _sb_sample_kernel(pt_ref, flag_ref, q_ref, a_ref, b_ref, *rest, pps, t_dec, first):
    k_refs = rest[:pps]
    v_refs = rest[pps:2 * pps]
    outs = rest[2 * pps:]
    rows = N_HEADS_B * t_dec
    w = N_HEADS_B * HEAD_DIM
    q = q_ref[0]
    tri = _tri(PAGE)
    if first:
        acc_ref, carry_ref, alive_ref = outs
        acc_o, carry_o = acc_ref.at[0], carry_ref.at[0]
    else:
        o_ref, acc_o, carry_o, alive_ref = outs
    j = pl.program_id(1)

    def block(kt, vt, mask):
        z = jnp.dot(q, kt.astype(BF16), preferred_element_type=F32)
        out, carry = _sb_block(z, mask, vt.astype(BF16), carry_o[...], tri, v_keys_on_lanes=True)
        carry_o[...] = carry
        acc_o[...] += out
        alive_ref[0] = (jnp.max(carry) > SB_DEAD).astype(I32)

    @pl.when(j == 0)
    def _():
        if first:
            carry_o[...] = jnp.zeros((rows, 1), F32)
            acc_o[...] = jnp.zeros((rows, w), F32)
            t_row = lax.broadcasted_iota(I32, (rows, 1), 0) % t_dec
            block(a_ref[0], b_ref[0], lax.broadcasted_iota(I32, (1, PAGE), 1) < t_row)
        else:
            acc_o[...] = a_ref[0]
            carry_o[...] = b_ref[0]
            alive_ref[0] = flag_ref[pl.program_id(0)]

    for r in range(pps):
        @pl.when(alive_ref[0] > 0)
        def _(r=r):
            block(k_refs[r][0, 0].reshape(w, PAGE), v_refs[r][0, 0].reshape(w, PAGE), None)

    if not first:
        @pl.when(j == pl.num_programs(1) - 1)
        def _():
            lane = lax.broadcasted_iota(I32, (1, w), 1)
            acc = acc_o[...]
            out = jnp.zeros((t_dec, w), F32)
            for h in range(N_HEADS_B):
                out = out + jnp.where(lane // HEAD_DIM == h, acc[h * t_dec:(h + 1) * t_dec, :], 0.0)
            o_ref[0] = out


def _block_diag_rows(q, n_heads):
    b, t, w = q.shape
    d = w // n_heads
    sel = (jnp.arange(w)[None, :] // d == jnp.arange(n_heads)[:, None]).astype(q.dtype)
    return (q[:, None, :, :] * sel[None, :, None, :]).reshape(b, n_heads * t, w)


def sb_sample(qb, kb, vb, pool_k, pool_v, layer, page_table, head_pages=8, pps=40):
    db, t, w = qb.shape
    n_pages = page_table.shape[1]
    head_pages = min(head_pages, n_pages)
    rest = n_pages - head_pages
    while rest % pps:
        pps -= 1
    pk, pv = _keys_on_lanes(pool_k), _keys_on_lanes(pool_v)
    q_bd = _block_diag_rows(qb * (HEAD_DIM ** -0.5), N_HEADS_B).astype(BF16)
    rows = N_HEADS_B * t
    page_block = (1, 1, N_HEADS_B, HEAD_DIM, PAGE)
    batch3 = lambda shape: pl.BlockSpec((1,) + shape, lambda b, j, pt, fl: (b, 0, 0))

    def head_spec(r):
        return pl.BlockSpec(page_block, lambda b, j, pt, fl: (layer, pt[b, n_pages - 1 - r], 0, 0, 0))

    ones = jnp.ones((db,), I32)
    acc, carry = pl.pallas_call(
        functools.partial(_sb_sample_kernel, pps=head_pages, t_dec=t, first=True),
        grid_spec=pltpu.PrefetchScalarGridSpec(
            num_scalar_prefetch=2,
            grid=(db, 1),
            in_specs=[batch3((rows, w)), batch3((w, PAGE)), batch3((w, PAGE))]
                     + [head_spec(r) for r in range(head_pages)] * 2,
            out_specs=[batch3((rows, w)), batch3((rows, 1))],
            scratch_shapes=[pltpu.SMEM((1,), I32)],
        ),
        out_shape=[jax.ShapeDtypeStruct((db, rows, w), F32), jax.ShapeDtypeStruct((db, rows, 1), F32)],
        compiler_params=_params(("parallel", "arbitrary")),
        name="sb_sample_head",
    )(page_table, ones, q_bd, _new_keys_on_lanes(kb), _new_keys_on_lanes(vb), *([pk] * head_pages), *([pv] * head_pages))

    alive = (jnp.max(carry, axis=(1, 2)) > SB_DEAD).astype(I32)

    def tail_spec(r):
        def index(b, j, pt, fl):
            page = pt[b, n_pages - 1 - head_pages - (j * pps + r)]
            return (layer, jnp.where(fl[b] > 0, page, 0), 0, 0, 0)
        return pl.BlockSpec(page_block, index)

    return pl.pallas_call(
        functools.partial(_sb_sample_kernel, pps=pps, t_dec=t, first=False),
        grid_spec=pltpu.PrefetchScalarGridSpec(
            num_scalar_prefetch=2,
            grid=(db, rest // pps),
            in_specs=[batch3((rows, w)), batch3((rows, w)), batch3((rows, 1))]
                     + [tail_spec(r) for r in range(pps)] * 2,
            out_specs=batch3((t, w)),
            scratch_shapes=[pltpu.VMEM((rows, w), F32), pltpu.VMEM((rows, 1), F32), pltpu.SMEM((1,), I32)],
        ),
        out_shape=jax.ShapeDtypeStruct((db, t, w), F32),
        compiler_params=_params(("parallel", "arbitrary")),
        name="sb_sample_tail",
    )(page_table, alive, q_bd, acc, carry, *([pk] * pps), *([pv] * pps))


def _dsa_sample_select_kernel(pt_ref, qi_ref, wi_ref, inew_ref, *rest, pps, t_dec, topk, past, cb):
    page_refs = rest[:pps]
    bias_ref, key_ref = rest[pps:]
    j = pl.program_id(1)
    qi = qi_ref[0]
    wi = wi_ref[0]
    nkeys = past + cb
    nblk = nkeys // cb

    def scores(kt):
        s = jnp.maximum(jnp.dot(qi, kt.astype(BF16), preferred_element_type=F32), 0.0) * wi
        sc = jnp.zeros((t_dec, kt.shape[1]), F32)
        for h in range(N_IDX_HEADS):
            sc = sc + s[h * t_dec:(h + 1) * t_dec, :]
        return _sortable_key(sc)

    c0 = pl.multiple_of(j * (pps * PAGE), pps * PAGE)
    key_ref[:, pl.ds(c0, pps * PAGE)] = scores(jnp.concatenate([page_refs[r][0, 0] for r in range(pps)], axis=1))

    @pl.when(j == pl.num_programs(1) - 1)
    def _():
        t_row = lax.broadcasted_iota(I32, (t_dec, 1), 0)
        lane = lax.broadcasted_iota(I32, (1, cb), 1)
        new = jnp.where(lax.broadcasted_iota(I32, (1, PAGE), 1) <= t_row, scores(inew_ref[0]), INT_MIN)
        key_ref[:, past:past + cb] = jnp.concatenate([new, jnp.full((t_dec, cb - PAGE), INT_MIN, I32)], axis=1)

        def count(pred):
            def body(c, acc):
                c0 = pl.multiple_of(c * cb, cb)
                hit = jnp.where(pred(key_ref[:, pl.ds(c0, cb)], c0 + lane), 1, 0)
                return acc + hit
            acc = lax.fori_loop(0, nblk, body, jnp.zeros((t_dec, cb), I32))
            return acc.sum(axis=1, keepdims=True)

        kq = jnp.minimum(past + t_row + 1, topk)
        none = jnp.zeros((t_dec, 1), I32)
        thr, found, sep, pending = _radix_select(
            lambda cand: count(lambda k, p: k >= cand), kq, 32, RADIX_GROUP, (t_dec, 1), none, none, lambda c: c)
        thr = jnp.where(found > 0, sep, thr)
        cut = _tie_cut(count, thr, kq, found, pending, (t_dec, 1), max(1, (nkeys - 1).bit_length()))

        def bias_body(c, carry):
            c0 = pl.multiple_of(c * cb, cb)
            k = key_ref[:, pl.ds(c0, cb)]
            sel = (k > thr) | ((k == thr) & (c0 + lane <= cut))
            bias_ref[0, :, pl.ds(c0, cb)] = jnp.where(sel, 0.0, MASK_BIAS).astype(F32)
            return carry

        lax.fori_loop(0, nblk, bias_body, 0)


def _dsa_sample_attend_kernel(pt_ref, q_ref, bias_ref, bnew_ref, knew_ref, vnew_ref, *rest, pps, t_dec):
    k_refs = rest[:pps]
    v_refs = rest[pps:2 * pps]
    o_ref, m_ref, l_ref, acc_ref = rest[2 * pps:]
    j = pl.program_id(1)
    kvw = N_KV_A * HEAD_DIM

    @pl.when(j == 0)
    def _():
        m_ref[...] = jnp.full_like(m_ref, M_INIT)
        l_ref[...] = jnp.zeros_like(l_ref)
        acc_ref[...] = jnp.zeros_like(acc_ref)

    def attend(kt, vt, bias):
        bias4 = jnp.concatenate([bias] * GROUP_A, axis=0)
        for c in range(N_KV_A):
            lg = jnp.dot(q_ref[0, c], kt, preferred_element_type=F32) + bias4
            m = m_ref[c]
            m_new = jnp.maximum(m, lg.max(axis=1, keepdims=True))
            alpha = jnp.exp2(m - m_new)
            p = jnp.exp2(lg - m_new)
            l_ref[c] = alpha * l_ref[c] + p.sum(axis=1, keepdims=True)
            acc_ref[c] = alpha * acc_ref[c] + _dot_nt(p.astype(BF16), vt)
            m_ref[c] = m_new

    kt = jnp.concatenate([k_refs[r][0, 0].reshape(kvw, PAGE) for r in range(pps)], axis=1).astype(BF16)
    vt = jnp.concatenate([v_refs[r][0, 0].reshape(kvw, PAGE) for r in range(pps)], axis=1).astype(BF16)
    attend(kt, vt, bias_ref[0])

    @pl.when(j == pl.num_programs(1) - 1)
    def _():
        attend(knew_ref[0].astype(BF16), vnew_ref[0].astype(BF16), bnew_ref[0])
        for c in range(N_KV_A):
            o_ref[0, c] = acc_ref[c] / l_ref[c]


def dsa_sample(qa, ka, va, qi, ki, wi, pool_k, pool_v, pool_ik, layer, page_table, topk, pps=32, cb=512):
    db, t = qa.shape[:2]
    n_pages = page_table.shape[1]
    past = n_pages * PAGE
    while n_pages % pps:
        pps //= 2
    cb = min(cb, pps * PAGE)
    nsteps = n_pages // pps
    kvw = N_KV_A * HEAD_DIM
    pk, pv, pik = _keys_on_lanes(pool_k), _keys_on_lanes(pool_v), _keys_on_lanes(pool_ik)

    def page_specs(block):
        zeros = (0,) * (len(block) - 2)
        return [pl.BlockSpec(block, lambda b, j, pt, r=r: (layer, pt[b, j * pps + r]) + zeros) for r in range(pps)]

    hrows = N_IDX_HEADS * t
    qi_r = qi.transpose(0, 2, 1, 3).reshape(db, hrows, IDX_DIM).astype(BF16)
    wi_r = wi.transpose(0, 2, 1).reshape(db, hrows, 1)
    nkeys = past + cb
    bias = pl.pallas_call(
        functools.partial(_dsa_sample_select_kernel, pps=pps, t_dec=t, topk=topk, past=past, cb=cb),
        grid_spec=pltpu.PrefetchScalarGridSpec(
            num_scalar_prefetch=1,
            grid=(db, nsteps),
            in_specs=[pl.BlockSpec((1, hrows, IDX_DIM), lambda b, j, pt: (b, 0, 0)),
                      pl.BlockSpec((1, hrows, 1), lambda b, j, pt: (b, 0, 0)),
                      pl.BlockSpec((1, IDX_DIM, PAGE), lambda b, j, pt: (b, 0, 0))]
                     + page_specs((1, 1, IDX_DIM, PAGE)),
            out_specs=pl.BlockSpec((1, t, nkeys), lambda b, j, pt: (b, 0, 0)),
            scratch_shapes=[pltpu.VMEM((t, nkeys), I32)],
        ),
        out_shape=jax.ShapeDtypeStruct((db, t, nkeys), F32),
        compiler_params=_params(("parallel", "arbitrary")),
        name="dsa_sample_select",
    )(page_table, qi_r, wi_r, _new_keys_on_lanes(ki), *([pik] * pps))

    q5 = (qa * (HEAD_DIM ** -0.5 * LOG2E)).reshape(db, t, N_KV_A, GROUP_A, HEAD_DIM).transpose(0, 2, 3, 1, 4)
    q5 = q5.reshape(db, N_KV_A, GROUP_A * t, HEAD_DIM)
    eye = jnp.eye(N_KV_A, dtype=F32)
    q_pad = (q5[:, :, :, None, :] * eye[None, :, None, :, None]).reshape(db, N_KV_A, GROUP_A * t, kvw).astype(BF16)
    rows = GROUP_A * t
    kv_block = (1, 1, N_KV_A, HEAD_DIM, PAGE)
    out = pl.pallas_call(
        functools.partial(_dsa_sample_attend_kernel, pps=pps, t_dec=t),
        grid_spec=pltpu.PrefetchScalarGridSpec(
            num_scalar_prefetch=1,
            grid=(db, nsteps),
            in_specs=[pl.BlockSpec((1, N_KV_A, rows, kvw), lambda b, j, pt: (b, 0, 0, 0)),
                      pl.BlockSpec((1, t, pps * PAGE), lambda b, j, pt: (b, 0, j)),
                      pl.BlockSpec((1, t, PAGE), lambda b, j, pt: (b, 0, past // PAGE)),
                      pl.BlockSpec((1, kvw, PAGE), lambda b, j, pt: (b, 0, 0)),
                      pl.BlockSpec((1, kvw, PAGE), lambda b, j, pt: (b, 0, 0))]
                     + page_specs(kv_block) + page_specs(kv_block),
            out_specs=pl.BlockSpec((1, N_KV_A, rows, kvw), lambda b, j, pt: (b, 0, 0, 0)),
            scratch_shapes=[pltpu.VMEM((N_KV_A, rows, 1), F32), pltpu.VMEM((N_KV_A, rows, 1), F32),
                            pltpu.VMEM((N_KV_A, rows, kvw), F32)],
        ),
        out_shape=jax.ShapeDtypeStruct((db, N_KV_A, rows, kvw), F32),
        compiler_params=_params(("parallel", "arbitrary")),
        name="dsa_sample_attend",
    )(page_table, q_pad, bias, bias, _new_keys_on_lanes(ka.reshape(db, t, kvw)), _new_keys_on_lanes(va.reshape(db, t, kvw)),
      *([pk] * pps), *([pv] * pps))
    o = out.reshape(db, N_KV_A, GROUP_A, t, N_KV_A, HEAD_DIM)
    o = jnp.stack([o[:, c, :, :, c, :] for c in range(N_KV_A)], axis=1)
    return o.transpose(0, 3, 1, 2, 4).reshape(db, t, N_HEADS_A * HEAD_DIM)


def _mem_kernel(x_ref, wq_ref, mk_ref, mv_ref, wo_ref, g_ref, b_ref, o_ref, *, alpha):
    x32 = x_ref[...].reshape(x_ref.shape[-2:])
    x = x32.astype(BF16)
    mk = mk_ref[...].reshape(mk_ref.shape[-2:]).astype(BF16)
    mv = mv_ref[...].reshape(mv_ref.shape[-2:]).astype(BF16)
    q = jnp.dot(x, wq_ref[...], preferred_element_type=F32).astype(BF16)
    outs = []
    for h in range(N_MEM_HEADS):
        sl = slice(h * MEM_HEAD_DIM, (h + 1) * MEM_HEAD_DIM)
        lg = _dot_nt(q[:, sl], mk[:, sl]) * (MEM_HEAD_DIM ** -0.5)
        p = jnp.exp(lg - lg.max(axis=1, keepdims=True))
        p = p / p.sum(axis=1, keepdims=True)
        outs.append(jnp.dot(p.astype(BF16), mv[:, sl], preferred_element_type=F32))
    o = jnp.concatenate(outs, axis=1).astype(BF16)
    attn = jnp.dot(o, wo_ref[...], preferred_element_type=F32)
    o_ref[...] = _residual_norm(x32, attn, alpha, g_ref[...], b_ref[...]).reshape(o_ref.shape)


def mem_attend(x, mk, mv, wq, wo, ln_g, ln_b, alpha, tm):
    b, t, d = x.shape
    n_mem, w = mk.shape[1:]
    row = pl.BlockSpec((1, d), lambda bi, i: (0, 0))
    return pl.pallas_call(
        functools.partial(_mem_kernel, alpha=alpha),
        grid=(b, t // tm),
        in_specs=[pl.BlockSpec((1, tm, d), lambda bi, i: (bi, i, 0)),
                  pl.BlockSpec(wq.shape, lambda bi, i: (0, 0)),
                  pl.BlockSpec((1, n_mem, w), lambda bi, i: (bi, 0, 0)),
                  pl.BlockSpec((1, n_mem, w), lambda bi, i: (bi, 0, 0)),
                  pl.BlockSpec(wo.shape, lambda bi, i: (0, 0)), row, row],
        out_specs=pl.BlockSpec((1, tm, d), lambda bi, i: (bi, i, 0)),
        out_shape=jax.ShapeDtypeStruct((b, t, d), F32),
        compiler_params=_params(("parallel", "parallel")),
        name="mem_attend",
    )(x, wq, mk, mv, wo, ln_g.reshape(1, d), ln_b.reshape(1, d))


def _split3_nt(w, x):
    wh = w.astype(BF16)
    wl = (w - wh.astype(F32)).astype(BF16)
    xh = x.astype(BF16)
    xl = (x - xh.astype(F32)).astype(BF16)
    return _dot_nt(wh, xh) + (_dot_nt(wh, xl) + _dot_nt(wl, xh))


def _router_kernel(x_ref, w_ref, b_ref, o_ref):
    logits = _split3_nt(w_ref[...], x_ref[...]) + b_ref[...]
    rows = [logits[e:e + 1, :] for e in range(N_EXPERTS)]
    mx = functools.reduce(jnp.maximum, rows)
    ex = [jnp.exp(r - mx) for r in rows]
    tot = functools.reduce(lambda a, b: a + b, ex)
    p = [e / tot for e in ex]
    gscore = []
    for g in range(N_GROUPS):
        a, b, c, d = p[4 * g:4 * g + 4]
        h1, l1, h2, l2 = jnp.maximum(a, b), jnp.minimum(a, b), jnp.maximum(c, d), jnp.minimum(c, d)
        gscore.append(jnp.maximum(h1, h2) + jnp.maximum(jnp.minimum(h1, h2), jnp.maximum(l1, l2)))
    best = gscore[0]
    group = jnp.zeros_like(best, dtype=I32)
    for g in range(1, N_GROUPS):
        better = gscore[g] > best
        best = jnp.where(better, gscore[g], best)
        group = jnp.where(better, g, group)
    sel = []
    for e in range(N_EXPERTS):
        g = e // EXPERTS_PER_GROUP
        rank = jnp.zeros_like(group)
        for o in range(g * EXPERTS_PER_GROUP, (g + 1) * EXPERTS_PER_GROUP):
            if o != e:
                ahead = (p[o] >= p[e]) if o < e else (p[o] > p[e])
                rank = rank + jnp.where(ahead, 1, 0)
        sel.append((group == g) & (rank < 2))
    top_sum = functools.reduce(lambda a, b: a + b, [jnp.where(s, pe, 0.0) for s, pe in zip(sel, p)])
    o_ref[...] = jnp.concatenate([jnp.where(s, pe / top_sum, 0.0) for s, pe in zip(sel, p)], axis=0)


def router(x, w_router_t, b_router, tm):
    n, d = x.shape
    return pl.pallas_call(
        _router_kernel,
        grid=(n // tm,),
        in_specs=[pl.BlockSpec((tm, d), lambda i: (i, 0)),
                  pl.BlockSpec((N_EXPERTS, d), lambda i: (0, 0)),
                  pl.BlockSpec((N_EXPERTS, 1), lambda i: (0, 0))],
        out_specs=pl.BlockSpec((N_EXPERTS, tm), lambda i: (0, i)),
        out_shape=jax.ShapeDtypeStruct((N_EXPERTS, n), F32),
        compiler_params=_params(("parallel",)),
        name="router",
    )(x, w_router_t, b_router.reshape(N_EXPERTS, 1))


def _moe_kernel(x_ref, wg_ref, wu_ref, wd_ref, comb_ref, g_ref, b_ref, o_ref, *, alpha):
    e = pl.program_id(1)
    x = x_ref[...].astype(BF16)
    g = jnp.dot(x, wg_ref[0], preferred_element_type=F32)
    u = jnp.dot(x, wu_ref[0], preferred_element_type=F32)
    h = (g * jax.nn.sigmoid(g)) * u
    y = jnp.dot(h.astype(BF16), wd_ref[0], preferred_element_type=F32)
    lane = lax.broadcasted_iota(I32, (1, N_EXPERTS), 1)
    ce = jnp.sum(jnp.where(lane == e, comb_ref[...], 0.0), axis=1, keepdims=True)

    @pl.when(e == 0)
    def _():
        o_ref[...] = ce * y

    @pl.when(e > 0)
    def _():
        o_ref[...] += ce * y

    @pl.when(e == N_EXPERTS - 1)
    def _():
        o_ref[...] = _residual_norm(x_ref[...], o_ref[...], alpha, g_ref[...], b_ref[...])


def moe_experts(x, comb, wg, wu, wd, ln_g, ln_b, alpha, tm):
    n, d = x.shape
    de = wg.shape[2]
    row = pl.BlockSpec((1, d), lambda i, e: (0, 0))
    return pl.pallas_call(
        functools.partial(_moe_kernel, alpha=alpha),
        grid=(n // tm, N_EXPERTS),
        in_specs=[pl.BlockSpec((tm, d), lambda i, e: (i, 0)),
                  pl.BlockSpec((1, d, de), lambda i, e: (e, 0, 0)),
                  pl.BlockSpec((1, d, de), lambda i, e: (e, 0, 0)),
                  pl.BlockSpec((1, de, d), lambda i, e: (e, 0, 0)),
                  pl.BlockSpec((tm, N_EXPERTS), lambda i, e: (i, 0)), row, row],
        out_specs=pl.BlockSpec((tm, d), lambda i, e: (i, 0)),
        out_shape=jax.ShapeDtypeStruct((n, d), F32),
        compiler_params=_params(("parallel", "arbitrary")),
        name="moe_experts",
    )(x, wg, wu, wd, comb, ln_g.reshape(1, d), ln_b.reshape(1, d))


def _rope(x, pos):
    rot = x.shape[-1] // 4
    half = rot // 2
    inv_freq = jnp.power(ROPE_THETA, -jnp.arange(half, dtype=F32) * 2.0 / rot)
    ang = pos.astype(F32)[:, None] * inv_freq[None, :]
    cos = jnp.cos(ang)[:, None, :]
    sin = jnp.sin(ang)[:, None, :]
    x1, x2 = x[..., :half], x[..., half:rot]
    return jnp.concatenate([x1 * cos - x2 * sin, x2 * cos + x1 * sin, x[..., rot:]], axis=-1)


def _pack_w_in(w_in_l):
    cols = [w_in_l[:, _SRC[n][0]:_SRC[n][0] + _SRC[n][1]] for n in _ORDER]
    packed = jnp.concatenate(cols, axis=1)
    return jnp.pad(packed, ((0, 0), (0, W_IN_PACKED - packed.shape[1]))).astype(BF16)


def _take(proj, name):
    o, n = _DST[name]
    return proj[..., o:o + n]


def _mixer_inputs(proj, pos):
    lead = proj.shape[:-1]
    hd = lambda name, h: _take(proj, name).reshape(lead + (h, HEAD_DIM))
    rp = lambda a: _rope(a, pos)
    qa, ka, va = rp(hd("qa", N_HEADS_A)), rp(hd("ka", N_KV_A)), hd("va", N_KV_A)
    qi = rp(hd("qi", N_IDX_HEADS))
    ki = rp(_take(proj, "ki")[..., None, :])[..., 0, :]
    wi = _take(proj, "wi")
    return qa, ka, va, qi, ki, wi, hd("kb", N_HEADS_B), hd("vb", N_HEADS_B)


def _tail(x1, mem_fn, lw, tm):
    x2 = mem_fn(x1)
    comb = router(x2, lw["w_router_t"], lw["b_router"], tm).T
    return moe_experts(x2, comb, lw["w_gate"], lw["w_up"], lw["w_down"], lw["ln3_g"], lw["ln3_b"], lw["alpha"], tm)


def kernel(x_prompt, x_sample, cache_a_k, cache_a_v, cache_idx_k, cache_b_k, cache_b_v, cache_mem_k, cache_mem_v,
           page_table, mem_prompt, w_in, w_pa, w_pb, w_o, ln1_g, ln1_b, w_cq, w_ck, w_cv, w_co, ln2_g, ln2_b,
           w_router, b_router, w_gate, w_up, w_down, ln3_g, ln3_b):
    depth = w_in.shape[0]
    bp, seq, d = x_prompt.shape
    db, t_dec, _ = x_sample.shape
    n_mem = mem_prompt.shape[1]
    past = page_table.shape[1] * PAGE
    k_prompt = min(MAX_SELECT, seq // 4)
    k_sample = min(MAX_SELECT, (past + t_dec) // 4)
    pos_p = jnp.arange(seq)
    pos_s = past + jnp.arange(t_dec)
    alpha = (2 * depth) ** 0.25
    tm_p = min(1024, seq)
    n_s = db * t_dec
    wm = N_MEM_HEADS * MEM_HEAD_DIM

    xp = x_prompt.reshape(bp * seq, d)
    xs = x_sample.reshape(n_s, d)
    outs = {k: [] for k in ("p_ak", "p_av", "p_ik", "p_bk", "p_bv", "p_mk", "p_mv", "s_ak", "s_av", "s_ik", "s_bk", "s_bv")}
    mem_bf = mem_prompt.reshape(bp * n_mem, d).astype(BF16)
    for l in range(depth):
        lw = dict(alpha=alpha, ln3_g=ln3_g[l], ln3_b=ln3_b[l], w_router_t=w_router.T, b_router=b_router,
                  w_gate=w_gate[l].astype(BF16), w_up=w_up[l].astype(BF16), w_down=w_down[l].astype(BF16))
        w_in_p = _pack_w_in(w_in[l])
        wpa, wpb, wo = w_pa[l].astype(BF16), w_pb[l].astype(BF16), w_o[l].astype(BF16)
        wcq, wco = w_cq[l].astype(BF16), w_co[l].astype(BF16)
        wckv = jnp.concatenate([w_ck[l], w_cv[l]], axis=1).astype(BF16)

        proj = matmul(xp, w_in_p, tm_p, 1024)
        proj_b = proj.reshape(bp, seq, W_IN_PACKED)
        qa, ka, va, qi, ki, wi, kb, vb = _mixer_inputs(proj_b, pos_p)
        ya = jnp.concatenate([dsa_prompt(qa[b], ka[b], va[b], qi[b], ki[b], wi[b], k_prompt) for b in range(bp)], axis=0)
        qb_s = (_take(proj_b, "qb") * (HEAD_DIM ** -0.5)).astype(BF16)
        kb_s = _take(proj_b, "kb").astype(BF16)
        vb_s = _take(proj_b, "vb").astype(BF16)
        yb = jnp.concatenate([sb_prompt(qb_s[b], kb_s[b], vb_s[b]) for b in range(bp)], axis=0)
        x1 = gated_merge(xp, ya, yb, proj, wpa, wpb, wo, ln1_g[l], ln1_b[l], alpha, tm_p)
        mkv = matmul(mem_bf, wckv, bp * n_mem, wm)
        mk, mv = mkv[:, :wm].reshape(bp, n_mem, wm), mkv[:, wm:].reshape(bp, n_mem, wm)
        mem_fn = lambda x: mem_attend(x.reshape(bp, seq, d), mk, mv, wcq, wco, ln2_g[l], ln2_b[l], alpha, tm_p).reshape(bp * seq, d)
        xp = _tail(x1, mem_fn, lw, tm_p)
        outs["p_ak"].append(ka); outs["p_av"].append(va); outs["p_ik"].append(ki)
        outs["p_bk"].append(kb); outs["p_bv"].append(vb)
        outs["p_mk"].append(mk.reshape(bp, n_mem, N_MEM_HEADS, MEM_HEAD_DIM))
        outs["p_mv"].append(mv.reshape(bp, n_mem, N_MEM_HEADS, MEM_HEAD_DIM))

        proj = matmul(xs, w_in_p, n_s, 1024)
        proj_b = proj.reshape(db, t_dec, W_IN_PACKED)
        qa, ka, va, qi, ki, wi, kb, vb = _mixer_inputs(proj_b, pos_s)
        ya = dsa_sample(qa, ka, va, qi, ki, wi, cache_a_k, cache_a_v, cache_idx_k, l, page_table, k_sample)
        yb = sb_sample(_take(proj_b, "qb"), _take(proj_b, "kb"), _take(proj_b, "vb"), cache_b_k, cache_b_v, l, page_table)
        x1 = gated_merge(xs, ya.reshape(n_s, -1), yb.reshape(n_s, -1), proj, wpa, wpb, wo, ln1_g[l], ln1_b[l], alpha, n_s)
        cmk = cache_mem_k[l].reshape(db, n_mem, wm)
        cmv = cache_mem_v[l].reshape(db, n_mem, wm)
        mem_fn = lambda x: mem_attend(x.reshape(db, t_dec, d), cmk, cmv, wcq, wco, ln2_g[l], ln2_b[l], alpha, t_dec).reshape(n_s, d)
        xs = _tail(x1, mem_fn, lw, n_s)
        outs["s_ak"].append(ka); outs["s_av"].append(va); outs["s_ik"].append(ki)
        outs["s_bk"].append(kb); outs["s_bv"].append(vb)

    st = lambda k: jnp.stack(outs[k])
    return (xp.reshape(bp, seq, d), xs.reshape(db, t_dec, d),
            st("p_ak"), st("p_av"), st("p_ik"), st("p_bk"), st("p_bv"), st("p_mk"), st("p_mv"),
            st("s_ak"), st("s_av"), st("s_ik"), st("s_bk"), st("s_bv"))
```

```python
import functools

import jax
import jax.numpy as jnp
from jax import lax
from jax.experimental import pallas as pl
from jax.experimental.pallas import tpu as pltpu

F32 = jnp.float32
BF16 = jnp.bfloat16
I32 = jnp.int32

HEAD_DIM = 64
N_HEADS_A = 8
N_KV_A = 2
GROUP_A = N_HEADS_A // N_KV_A
N_IDX_HEADS = 8
IDX_DIM = 64
N_HEADS_B = 8
MAX_SELECT = 256
ROPE_THETA = 500000.0
PAGE = 128
N_MEM_HEADS = 4
MEM_HEAD_DIM = 128
N_EXPERTS = 16
N_GROUPS = 4
EXPERTS_PER_GROUP = 4
LN_EPS = 1e-5

LANES = 128
SUBLANES = 8
VMEM_LIMIT = 56 * 2**20
N_ACC = 4

INT_MIN = -(2**31)
INT_MAX = 2**31 - 1
M_INIT = -1e30
MASK_BIAS = -2e30
NO_LIMIT = 1e9
LOG2E = 1.4426950408889634
RADIX_GROUP = 4
LAZY_SUM_LIMIT = 2.0 ** 30
SB_DEAD = -104.0

_SRC = dict(qa=(0, 512), ka=(512, 128), va=(640, 128), qi=(768, 512), ki=(1280, 64), wi=(1344, 8),
            qb=(1352, 512), kb=(1864, 512), vb=(2376, 512), ga=(2888, 1024), gb=(3912, 1024))
_ORDER = ("ga", "gb", "qa", "qi", "qb", "kb", "vb", "ka", "va", "ki", "wi")
_DST = {}
_off = 0
for _n in _ORDER:
    _DST[_n] = (_off, _SRC[_n][1])
    _off += _SRC[_n][1]
W_IN_PACKED = 5120


def _params(sem):
    return pltpu.CompilerParams(dimension_semantics=sem, vmem_limit_bytes=VMEM_LIMIT)


def _dot_nt(a, b):
    return lax.dot_general(a, b, (((1,), (1,)), ((), ())), preferred_element_type=F32)


def _mm_kernel(x_ref, w_ref, o_ref):
    o_ref[...] = jnp.dot(x_ref[...].astype(BF16), w_ref[...], preferred_element_type=F32).astype(o_ref.dtype)


def matmul(x, w, tm, tn, out_dtype=F32):
    m, k = x.shape
    n = w.shape[1]
    return pl.pallas_call(
        _mm_kernel,
        grid=(m // tm, n // tn),
        in_specs=[pl.BlockSpec((tm, k), lambda i, j: (i, 0)), pl.BlockSpec((k, tn), lambda i, j: (0, j))],
        out_specs=pl.BlockSpec((tm, tn), lambda i, j: (i, j)),
        out_shape=jax.ShapeDtypeStruct((m, n), out_dtype),
        compiler_params=_params(("parallel", "arbitrary")),
        name="matmul",
    )(x, w)


def _residual_norm(x, y, alpha, g, b):
    v = alpha * x + y
    mu = jnp.mean(v, axis=-1, keepdims=True)
    dev = v - mu
    var = jnp.mean(dev * dev, axis=-1, keepdims=True)
    return dev * lax.rsqrt(var + LN_EPS) * g + b


def _merge_kernel(x_ref, ya_ref, yb_ref, ga_ref, gb_ref, wpa_ref, wpb_ref, wo_ref, g_ref, b_ref, o_ref, *, alpha):
    a = jnp.dot(ya_ref[...].astype(BF16), wpa_ref[...], preferred_element_type=F32)
    b = jnp.dot(yb_ref[...].astype(BF16), wpb_ref[...], preferred_element_type=F32)
    merged = jax.nn.sigmoid(ga_ref[...]) * a + jax.nn.sigmoid(gb_ref[...]) * b
    mix = jnp.dot(merged.astype(BF16), wo_ref[...], preferred_element_type=F32)
    o_ref[...] = _residual_norm(x_ref[...], mix, alpha, g_ref[...], b_ref[...])


def gated_merge(x, ya, yb, proj, wpa, wpb, wo, ln_g, ln_b, alpha, tm):
    m = ya.shape[0]
    d = wo.shape[1]
    full = lambda a: pl.BlockSpec(a.shape, lambda i: (0, 0))
    row = pl.BlockSpec((1, d), lambda i: (0, 0))
    return pl.pallas_call(
        functools.partial(_merge_kernel, alpha=alpha),
        grid=(m // tm,),
        in_specs=[pl.BlockSpec((tm, d), lambda i: (i, 0)),
                  pl.BlockSpec((tm, ya.shape[1]), lambda i: (i, 0)),
                  pl.BlockSpec((tm, yb.shape[1]), lambda i: (i, 0)),
                  pl.BlockSpec((tm, d), lambda i: (i, _DST["ga"][0] // d)),
                  pl.BlockSpec((tm, d), lambda i: (i, _DST["gb"][0] // d)),
                  full(wpa), full(wpb), full(wo), row, row],
        out_specs=pl.BlockSpec((tm, d), lambda i: (i, 0)),
        out_shape=jax.ShapeDtypeStruct((m, d), F32),
        compiler_params=_params(("parallel",)),
        name="gated_merge",
    )(x, ya, yb, proj, proj, wpa, wpb, wo, ln_g.reshape(1, d), ln_b.reshape(1, d))


def _sortable_key(score):
    bits = lax.bitcast_convert_type(score, I32)
    key = jnp.where(bits < 0, bits ^ INT_MAX, bits)
    return jnp.where(key == -1, 0, key)


def _radix_select(count_ge, kq, width, group, shape, found, sep, full_key):
    low = -(1 << (width - 1))

    def probe(cand, prefix, found, sep):
        cnt = count_ge(cand)
        sep = jnp.where((cnt == kq) & (found == 0), full_key(cand), sep)
        return jnp.where(cnt >= kq, cand, prefix), jnp.where(cnt == kq, 1, found), sep

    prefix, found, sep = probe(jnp.zeros(shape, I32), jnp.full(shape, low, I32), found, sep)
    n_groups = -(-(width - 1) // group)

    def group_cond(st):
        return (st[0] < n_groups) & (st[4] > 0)

    def group_body(st):
        gi, prefix, found, sep, _ = st

        def bit_body(j, inner):
            prefix, found, sep = inner
            shift = width - 2 - (gi * group + j)
            bit = jnp.where(shift >= 0, jnp.left_shift(jnp.int32(1), jnp.maximum(shift, 0)), 0)
            return probe(prefix | bit, prefix, found, sep)

        prefix, found, sep = lax.fori_loop(0, group, bit_body, (prefix, found, sep))
        return gi + 1, prefix, found, sep, jnp.max(1 - found)

    _, prefix, found, sep, pending = lax.while_loop(
        group_cond, group_body, (jnp.int32(0), prefix, found, sep, jnp.max(1 - found)))
    return prefix, found, sep, pending


def _tie_cut(count, thr, kq, found, pending, shape, idx_bits):
    def tie_search():
        need = kq - count(lambda k, p: k > thr)

        def tie_body(bi, cut):
            cand = cut | jnp.left_shift(jnp.int32(1), idx_bits - 1 - bi)
            below = count(lambda k, p: (k == thr) & (p < cand))
            return jnp.where(below < need, cand, cut)

        cut = lax.fori_loop(0, idx_bits, tie_body, jnp.zeros(shape, I32))
        return jnp.where(found > 0, INT_MAX, cut)

    return lax.cond(pending > 0, tie_search, lambda: jnp.full(shape, INT_MAX, I32))


def _dsa_prompt_kernel(kidx_ref, qi_ref, wi_ref, k_ref, vt_ref, q_ref, o_ref, key_ref, *, tq, kc, sub, topk):
    i = pl.program_id(0)
    nchunk = ((i + 1) * tq + kc - 1) // kc
    qpos = i * tq + lax.broadcasted_iota(I32, (1, tq), 1)
    row_iota = lax.broadcasted_iota(I32, (kc, 1), 0)

    qi = qi_ref[0]
    wi = wi_ref[0]

    def idx_body(c, carry):
        for j in range(kc // sub):
            r0 = pl.multiple_of(c * kc + j * sub, sub)
            s = jnp.dot(kidx_ref[pl.ds(r0, sub), :], qi, preferred_element_type=F32)
            sc = jnp.zeros((sub, tq), F32)
            for h in range(N_IDX_HEADS):
                sc = sc + jnp.maximum(s[:, h * tq:(h + 1) * tq], 0.0) * wi[h:h + 1, :]
            key = jnp.where(r0 + row_iota[:sub] <= qpos, _sortable_key(sc), INT_MIN)
            key_ref[pl.ds(r0, sub), :] = key
        return carry

    lax.fori_loop(0, nchunk, idx_body, 0)

    def count(pred):
        def body(c, acc):
            r0 = pl.multiple_of(c * kc, kc)
            hit = jnp.where(pred(key_ref[pl.ds(r0, kc), :], r0 + row_iota), 1, 0)
            return acc + hit.reshape(kc // (N_ACC * SUBLANES), N_ACC * SUBLANES, tq).sum(axis=0)
        acc = lax.fori_loop(0, nchunk, body, jnp.zeros((N_ACC * SUBLANES, tq), I32))
        return acc.sum(axis=0, keepdims=True)

    kq = jnp.minimum(qpos + 1, topk)
    none = jnp.zeros((1, tq), I32)
    thr, found, sep, pending = _radix_select(
        lambda cand: count(lambda k, p: k >= cand), kq, 32, RADIX_GROUP, (1, tq), none, none, lambda c: c)
    thr = jnp.where(found > 0, sep, thr)
    n_gt = lax.cond(pending > 0, lambda: count(lambda k, p: k > thr), lambda: none)
    need = jnp.where(found > 0, NO_LIMIT, (kq - n_gt).astype(F32))
    earlier = jnp.where(lax.broadcasted_iota(I32, (sub, sub), 1) < lax.broadcasted_iota(I32, (sub, sub), 0), 1.0, 0.0).astype(BF16)

    def bias_body(c, ties_before):
        for j in range(kc // sub):
            r0 = pl.multiple_of(c * kc + j * sub, sub)
            k = key_ref[pl.ds(r0, sub), :]
            tie = jnp.where(k == thr, 1.0, 0.0)
            before = jnp.dot(earlier, tie.astype(BF16), preferred_element_type=F32) + ties_before
            sel = (k > thr) | ((k == thr) & (before < need))
            key_ref[pl.ds(r0, sub), :] = lax.bitcast_convert_type(jnp.where(sel, 0.0, MASK_BIAS).astype(F32), I32)
            ties_before = ties_before + tie.sum(axis=0, keepdims=True)
        return ties_before

    lax.fori_loop(0, nchunk, bias_body, jnp.zeros((1, tq), F32))

    nq = GROUP_A * tq

    def att_body(ch, carry):
        def logits(c, r0, n):
            bias = lax.bitcast_convert_type(key_ref[pl.ds(r0, n), :], F32)
            bias = jnp.concatenate([bias] * GROUP_A, axis=1)
            return jnp.dot(k_ref[pl.ds(r0, n), :], q_ref[0, c], preferred_element_type=F32) + bias

        def pv(c, p, r0, n):
            vt = vt_ref[c * HEAD_DIM:(c + 1) * HEAD_DIM, pl.ds(r0, n)]
            return jnp.dot(vt, p.astype(BF16), preferred_element_type=F32)

        r0 = pl.multiple_of(ch * kc, kc)
        lazy, overrun = [], jnp.float32(0.0)
        lgs = [logits(c, r0, kc) for c in range(N_KV_A)]
        for c in range(N_KV_A):
            m, l, acc = carry[c]
            p = jnp.exp2(lgs[c] - m)
            l = l + p.sum(axis=0, keepdims=True)
            lazy.append((m, l, acc + pv(c, p, r0, kc)))
            overrun = jnp.maximum(overrun, jnp.max(jnp.where(l < LAZY_SUM_LIMIT, 0.0, 1.0)))

        def keep_lazy():
            return tuple(lazy)

        def redo_with_chunk_max():
            out = []
            for c in range(N_KV_A):
                m, l, acc = carry[c]
                lg = logits(c, r0, kc)
                m_new = jnp.maximum(m, lg.max(axis=0, keepdims=True))
                alpha = jnp.exp2(m - m_new)
                p = jnp.exp2(lg - m_new)
                out.append((m_new, alpha * l + p.sum(axis=0, keepdims=True), alpha * acc + pv(c, p, r0, kc)))
            return tuple(out)

        return lax.cond(overrun > 0.0, redo_with_chunk_max, keep_lazy)

    init = (jnp.full((1, nq), M_INIT, F32), jnp.zeros((1, nq), F32), jnp.zeros((HEAD_DIM, nq), F32))
    res = lax.fori_loop(0, nchunk, att_body, (init,) * N_KV_A)
    for c in range(N_KV_A):
        o_ref[0, c] = res[c][2] / res[c][1]


def dsa_prompt(qa, ka, va, qi, ki, wi, topk, tq=128, kc=1024, sub=128):
    s = qa.shape[0]
    kc = min(kc, s)
    nt = s // tq
    kidx = ki.astype(BF16)
    qi_t = qi.reshape(nt, tq, N_IDX_HEADS, IDX_DIM).transpose(0, 3, 2, 1).reshape(nt, IDX_DIM, N_IDX_HEADS * tq).astype(BF16)
    wi_t = wi.reshape(nt, tq, N_IDX_HEADS).transpose(0, 2, 1)
    k2 = ka.reshape(s, N_KV_A * HEAD_DIM).astype(BF16)
    v2t = va.reshape(s, N_KV_A * HEAD_DIM).T.astype(BF16)
    q5 = (qa * (HEAD_DIM ** -0.5 * LOG2E)).reshape(nt, tq, N_KV_A, GROUP_A, HEAD_DIM).transpose(0, 2, 4, 3, 1)
    q5 = q5.reshape(nt, N_KV_A, HEAD_DIM, GROUP_A * tq)
    eye = jnp.eye(N_KV_A, dtype=F32)
    q_pad = (eye[None, :, :, None, None] * q5[:, :, None, :, :]).reshape(nt, N_KV_A, N_KV_A * HEAD_DIM, GROUP_A * tq)
    q_pad = q_pad.astype(BF16)
    out = pl.pallas_call(
        functools.partial(_dsa_prompt_kernel, tq=tq, kc=kc, sub=min(sub, kc), topk=topk),
        grid=(nt,),
        in_specs=[pl.BlockSpec((s, IDX_DIM), lambda i: (0, 0)),
                  pl.BlockSpec((1, IDX_DIM, N_IDX_HEADS * tq), lambda i: (i, 0, 0)),
                  pl.BlockSpec((1, N_IDX_HEADS, tq), lambda i: (i, 0, 0)),
                  pl.BlockSpec((s, N_KV_A * HEAD_DIM), lambda i: (0, 0)),
                  pl.BlockSpec((N_KV_A * HEAD_DIM, s), lambda i: (0, 0)),
                  pl.BlockSpec((1, N_KV_A, N_KV_A * HEAD_DIM, GROUP_A * tq), lambda i: (i, 0, 0, 0))],
        out_specs=pl.BlockSpec((1, N_KV_A, HEAD_DIM, GROUP_A * tq), lambda i: (i, 0, 0, 0)),
        out_shape=jax.ShapeDtypeStruct((nt, N_KV_A, HEAD_DIM, GROUP_A * tq), F32),
        scratch_shapes=[pltpu.VMEM((s, tq), I32)],
        compiler_params=_params(("parallel",)),
        name="dsa_prompt",
    )(kidx, qi_t, wi_t, k2, v2t, q_pad)
    out = out.reshape(nt, N_KV_A, HEAD_DIM, GROUP_A, tq).transpose(0, 4, 1, 3, 2)
    return out.reshape(s, N_HEADS_A * HEAD_DIM)


def _tri(n):
    return jnp.where(lax.broadcasted_iota(I32, (n, n), 0) > lax.broadcasted_iota(I32, (n, n), 1), 1.0, 0.0).astype(BF16)


def _sb_blocks(zs, masks, vs, carries, tri, v_keys_on_lanes=False):
    n = len(zs)
    log_beta, log_keep, parts = [], [], []
    for z, mask in zip(zs, masks):
        soft = jnp.log1p(jnp.exp(-jnp.abs(z)))
        log_beta.append(jnp.minimum(z, 0.0) - soft)
        lk = -jnp.maximum(z, 0.0) - soft
        if mask is not None:
            lk = jnp.where(mask, lk, 0.0)
        log_keep.append(lk)
        hi = lk.astype(BF16)
        r1 = lk - hi.astype(F32)
        mid = r1.astype(BF16)
        parts.append((hi, mid, (r1 - mid.astype(F32)).astype(BF16)))
    after = [(jnp.dot(hi, tri, preferred_element_type=F32) + jnp.dot(mid, tri, preferred_element_type=F32)
              + jnp.dot(lo, tri, preferred_element_type=F32)) for hi, mid, lo in parts]
    weights = []
    for i in range(n):
        a = jnp.exp(log_beta[i] + (after[i] + carries[i]))
        if masks[i] is not None:
            a = jnp.where(masks[i], a, 0.0)
        weights.append(a.astype(BF16))
    outs = [_dot_nt(a, v) if v_keys_on_lanes else jnp.dot(a, v, preferred_element_type=F32)
            for a, v in zip(weights, vs)]
    return outs, [c + jnp.sum(lk, axis=1, keepdims=True) for c, lk in zip(carries, log_keep)]


def _sb_block(z, mask, v, carry, tri, v_keys_on_lanes=False):
    outs, carries = _sb_blocks([z], [mask], [v], [carry], tri, v_keys_on_lanes)
    return outs[0], carries[0]


def _sb_prompt_kernel(q_ref, k_ref, v_ref, o_ref, *, tq, nsub):
    i = pl.program_id(1)
    lane = lax.broadcasted_iota(I32, (1, 2 * HEAD_DIM), 1)
    row = lax.broadcasted_iota(I32, (tq, 1), 0)
    key_iota = lax.broadcasted_iota(I32, (1, tq), 1)
    tri = _tri(tq)
    q_all = q_ref[...]
    qh = [[jnp.where(lane // HEAD_DIM == hh, q_all[s * tq:(s + 1) * tq], jnp.zeros((tq, 2 * HEAD_DIM), q_all.dtype))
           for hh in range(2)] for s in range(nsub)]

    def cond(st):
        return st[1] > 0

    def body(st):
        d, _, carries, accs = st
        zs, masks, vs, older = [], [], [], []
        for s in range(nsub):
            tile = i * nsub + s
            c = tile - d
            r0 = pl.multiple_of(jnp.maximum(c, 0) * tq, tq)
            kblk = k_ref[pl.ds(r0, tq), :]
            mask = ((r0 + key_iota) < (tile * tq + row)) & (c >= 0)
            for hh in range(2):
                zs.append(_dot_nt(qh[s][hh], kblk))
                masks.append(mask)
                vs.append(v_ref[pl.ds(r0, tq), :])
                older.append(c >= 1)
        outs, new_c = _sb_blocks(zs, masks, vs, list(carries), tri)
        live = jnp.full((tq, 1), SB_DEAD, F32)
        for carry, more in zip(new_c, older):
            live = jnp.maximum(live, jnp.where(more, carry, SB_DEAD))
        alive = (jnp.max(live) > SB_DEAD).astype(I32)
        return d + 1, alive, tuple(new_c), tuple(a + o for a, o in zip(accs, outs))

    zc = jnp.zeros((tq, 1), F32)
    za = jnp.zeros((tq, 2 * HEAD_DIM), F32)
    accs = lax.while_loop(cond, body, (jnp.int32(0), jnp.int32(1), (zc,) * (2 * nsub), (za,) * (2 * nsub)))[3]
    for s in range(nsub):
        o_ref[s * tq:(s + 1) * tq, :] = jnp.where(lane < HEAD_DIM, accs[2 * s], accs[2 * s + 1])


def sb_prompt(q, k, v, tq=128, nsub=4):
    s, w = k.shape
    npair = w // (2 * HEAD_DIM)
    nsub = min(nsub, s // tq)
    rows = nsub * tq
    return pl.pallas_call(
        functools.partial(_sb_prompt_kernel, tq=tq, nsub=nsub),
        grid=(npair, s // rows),
        in_specs=[pl.BlockSpec((rows, 2 * HEAD_DIM), lambda h, i: (i, h)),
                  pl.BlockSpec((s, 2 * HEAD_DIM), lambda h, i: (0, h)),
                  pl.BlockSpec((s, 2 * HEAD_DIM), lambda h, i: (0, h))],
        out_specs=pl.BlockSpec((rows, 2 * HEAD_DIM), lambda h, i: (i, h)),
        out_shape=jax.ShapeDtypeStruct((s, w), F32),
        compiler_params=_params(("parallel", "parallel")),
        name="sb_prompt",
    )(q, k, v)


def _keys_on_lanes(pool):
    nd = pool.ndim
    return pool.transpose((0, 1) + tuple(range(3, nd)) + (2,))


def _new_keys_on_lanes(a):
    a = a.transpose(0, 2, 1)
    return jnp.pad(a, ((0, 0), (0, 0), (0, PAGE - a.shape[2])))


def _sb_sample_kernel(pt_ref, flag_ref, q_ref, a_ref, b_ref, *rest, pps, t_dec, first):
    k_refs = rest[:pps]
    v_refs = rest[pps:2 * pps]
    outs = rest[2 * pps:]
    rows = N_HEADS_B * t_dec
    w = N_HEADS_B * HEAD_DIM
    q = q_ref[0]
    tri = _tri(PAGE)
    if first:
        acc_ref, carry_ref, alive_ref = outs
        acc_o, carry_o = acc_ref.at[0], carry_ref.at[0]
    else:
        o_ref, acc_o, carry_o, alive_ref = outs
    j = pl.program_id(1)

    def block(kt, vt, mask):
        z = jnp.dot(q, kt.astype(BF16), preferred_element_type=F32)
        out, carry = _sb_block(z, mask, vt.astype(BF16), carry_o[...], tri, v_keys_on_lanes=True)
        carry_o[...] = carry
        acc_o[...] += out
        alive_ref[0] = (jnp.max(carry) > SB_DEAD).astype(I32)

    @pl.when(j == 0)
    def _():
        if first:
            carry_o[...] = jnp.zeros((rows, 1), F32)
            acc_o[...] = jnp.zeros((rows, w), F32)
            t_row = lax.broadcasted_iota(I32, (rows, 1), 0) % t_dec
            block(a_ref[0], b_ref[0], lax.broadcasted_iota(I32, (1, PAGE), 1) < t_row)
        else:
            acc_o[...] = a_ref[0]
            carry_o[...] = b_ref[0]
            alive_ref[0] = flag_ref[pl.program_id(0)]

    for r in range(pps):
        @pl.when(alive_ref[0] > 0)
        def _(r=r):
            block(k_refs[r][0, 0].reshape(w, PAGE), v_refs[r][0, 0].reshape(w, PAGE), None)

    if not first:
        @pl.when(j == pl.num_programs(1) - 1)
        def _():
            lane = lax.broadcasted_iota(I32, (1, w), 1)
            acc = acc_o[...]
            out = jnp.zeros((t_dec, w), F32)
            for h in range(N_HEADS_B):
                out = out + jnp.where(lane // HEAD_DIM == h, acc[h * t_dec:(h + 1) * t_dec, :], 0.0)
            o_ref[0] = out


def _block_diag_rows(q, n_heads):
    b, t, w = q.shape
    d = w // n_heads
    sel = (jnp.arange(w)[None, :] // d == jnp.arange(n_heads)[:, None]).astype(q.dtype)
    return (q[:, None, :, :] * sel[None, :, None, :]).reshape(b, n_heads * t, w)


def sb_sample(qb, kb, vb, pool_k, pool_v, layer, page_table, head_pages=8, pps=40):
    db, t, w = qb.shape
    n_pages = page_table.shape[1]
    head_pages = min(head_pages, n_pages)
    rest = n_pages - head_pages
    while rest % pps:
        pps -= 1
    pk, pv = _keys_on_lanes(pool_k), _keys_on_lanes(pool_v)
    q_bd = _block_diag_rows(qb * (HEAD_DIM ** -0.5), N_HEADS_B).astype(BF16)
    rows = N_HEADS_B * t
    page_block = (1, 1, N_HEADS_B, HEAD_DIM, PAGE)
    batch3 = lambda shape: pl.BlockSpec((1,) + shape, lambda b, j, pt, fl: (b, 0, 0))

    def head_spec(r):
        return pl.BlockSpec(page_block, lambda b, j, pt, fl: (layer, pt[b, n_pages - 1 - r], 0, 0, 0))

    ones = jnp.ones((db,), I32)
    acc, carry = pl.pallas_call(
        functools.partial(_sb_sample_kernel, pps=head_pages, t_dec=t, first=True),
        grid_spec=pltpu.PrefetchScalarGridSpec(
            num_scalar_prefetch=2,
            grid=(db, 1),
            in_specs=[batch3((rows, w)), batch3((w, PAGE)), batch3((w, PAGE))]
                     + [head_spec(r) for r in range(head_pages)] * 2,
            out_specs=[batch3((rows, w)), batch3((rows, 1))],
            scratch_shapes=[pltpu.SMEM((1,), I32)],
        ),
        out_shape=[jax.ShapeDtypeStruct((db, rows, w), F32), jax.ShapeDtypeStruct((db, rows, 1), F32)],
        compiler_params=_params(("parallel", "arbitrary")),
        name="sb_sample_head",
    )(page_table, ones, q_bd, _new_keys_on_lanes(kb), _new_keys_on_lanes(vb), *([pk] * head_pages), *([pv] * head_pages))

    alive = (jnp.max(carry, axis=(1, 2)) > SB_DEAD).astype(I32)

    def tail_spec(r):
        def index(b, j, pt, fl):
            page = pt[b, n_pages - 1 - head_pages - (j * pps + r)]
            return (layer, jnp.where(fl[b] > 0, page, 0), 0, 0, 0)
        return pl.BlockSpec(page_block, index)

    def run_tail():
        return pl.pallas_call(
            functools.partial(_sb_sample_kernel, pps=pps, t_dec=t, first=False),
            grid_spec=pltpu.PrefetchScalarGridSpec(
                num_scalar_prefetch=2,
                grid=(db, rest // pps),
                in_specs=[batch3((rows, w)), batch3((rows, w)), batch3((rows, 1))]
                         + [tail_spec(r) for r in range(pps)] * 2,
                out_specs=batch3((t, w)),
                scratch_shapes=[pltpu.VMEM((rows, w), F32), pltpu.VMEM((rows, 1), F32), pltpu.SMEM((1,), I32)],
            ),
            out_shape=jax.ShapeDtypeStruct((db, t, w), F32),
            compiler_params=_params(("parallel", "arbitrary")),
            name="sb_sample_tail",
        )(page_table, alive, q_bd, acc, carry, *([pk] * pps), *([pv] * pps))

    def head_only():
        blocks = acc.reshape(db, N_HEADS_B, t, N_HEADS_B, HEAD_DIM)
        return jnp.stack([blocks[:, h, :, h, :] for h in range(N_HEADS_B)], axis=2).reshape(db, t, w)

    return lax.cond(jnp.max(alive) > 0, run_tail, head_only)


def _dsa_sample_select_kernel(pt_ref, qi_ref, wi_ref, inew_ref, *rest, pps, t_dec, topk, past, cb):
    page_refs = rest[:pps]
    bias_ref, key_ref = rest[pps:]
    j = pl.program_id(1)
    qi = qi_ref[0]
    wi = wi_ref[0]
    nkeys = past + cb
    nblk = nkeys // cb

    def scores(kt):
        s = jnp.maximum(jnp.dot(qi, kt.astype(BF16), preferred_element_type=F32), 0.0) * wi
        sc = jnp.zeros((t_dec, kt.shape[1]), F32)
        for h in range(N_IDX_HEADS):
            sc = sc + s[h * t_dec:(h + 1) * t_dec, :]
        return _sortable_key(sc)

    c0 = pl.multiple_of(j * (pps * PAGE), pps * PAGE)
    key_ref[:, pl.ds(c0, pps * PAGE)] = scores(jnp.concatenate([page_refs[r][0, 0] for r in range(pps)], axis=1))

    @pl.when(j == pl.num_programs(1) - 1)
    def _():
        t_row = lax.broadcasted_iota(I32, (t_dec, 1), 0)
        lane = lax.broadcasted_iota(I32, (1, cb), 1)
        new = jnp.where(lax.broadcasted_iota(I32, (1, PAGE), 1) <= t_row, scores(inew_ref[0]), INT_MIN)
        key_ref[:, past:past + cb] = jnp.concatenate([new, jnp.full((t_dec, cb - PAGE), INT_MIN, I32)], axis=1)

        def count(pred):
            def body(c, acc):
                c0 = pl.multiple_of(c * cb, cb)
                hit = jnp.where(pred(key_ref[:, pl.ds(c0, cb)], c0 + lane), 1, 0)
                return acc + hit
            acc = lax.fori_loop(0, nblk, body, jnp.zeros((t_dec, cb), I32))
            return acc.sum(axis=1, keepdims=True)

        kq = jnp.minimum(past + t_row + 1, topk)
        none = jnp.zeros((t_dec, 1), I32)
        thr, found, sep, pending = _radix_select(
            lambda cand: count(lambda k, p: k >= cand), kq, 32, RADIX_GROUP, (t_dec, 1), none, none, lambda c: c)
        thr = jnp.where(found > 0, sep, thr)
        cut = _tie_cut(count, thr, kq, found, pending, (t_dec, 1), max(1, (nkeys - 1).bit_length()))

        def bias_body(c, carry):
            c0 = pl.multiple_of(c * cb, cb)
            k = key_ref[:, pl.ds(c0, cb)]
            sel = (k > thr) | ((k == thr) & (c0 + lane <= cut))
            bias_ref[0, :, pl.ds(c0, cb)] = jnp.where(sel, 0.0, MASK_BIAS).astype(F32)
            return carry

        lax.fori_loop(0, nblk, bias_body, 0)


def _dsa_sample_attend_kernel(pt_ref, q_ref, bias_ref, bnew_ref, knew_ref, vnew_ref, *rest, pps, t_dec):
    k_refs = rest[:pps]
    v_refs = rest[pps:2 * pps]
    o_ref, m_ref, l_ref, acc_ref = rest[2 * pps:]
    j = pl.program_id(1)
    kvw = N_KV_A * HEAD_DIM

    @pl.when(j == 0)
    def _():
        m_ref[...] = jnp.full_like(m_ref, M_INIT)
        l_ref[...] = jnp.zeros_like(l_ref)
        acc_ref[...] = jnp.zeros_like(acc_ref)

    def attend(kt, vt, bias):
        bias4 = jnp.concatenate([bias] * GROUP_A, axis=0)
        for c in range(N_KV_A):
            lg = jnp.dot(q_ref[0, c], kt, preferred_element_type=F32) + bias4
            m = m_ref[c]
            m_new = jnp.maximum(m, lg.max(axis=1, keepdims=True))
            alpha = jnp.exp2(m - m_new)
            p = jnp.exp2(lg - m_new)
            l_ref[c] = alpha * l_ref[c] + p.sum(axis=1, keepdims=True)
            acc_ref[c] = alpha * acc_ref[c] + _dot_nt(p.astype(BF16), vt)
            m_ref[c] = m_new

    kt = jnp.concatenate([k_refs[r][0, 0].reshape(kvw, PAGE) for r in range(pps)], axis=1).astype(BF16)
    vt = jnp.concatenate([v_refs[r][0, 0].reshape(kvw, PAGE) for r in range(pps)], axis=1).astype(BF16)
    attend(kt, vt, bias_ref[0])

    @pl.when(j == pl.num_programs(1) - 1)
    def _():
        attend(knew_ref[0].astype(BF16), vnew_ref[0].astype(BF16), bnew_ref[0])
        for c in range(N_KV_A):
            o_ref[0, c] = acc_ref[c] / l_ref[c]


def dsa_sample(qa, ka, va, qi, ki, wi, pool_k, pool_v, pool_ik, layer, page_table, topk, pps=32, cb=512):
    db, t = qa.shape[:2]
    n_pages = page_table.shape[1]
    past = n_pages * PAGE
    while n_pages % pps:
        pps //= 2
    cb = min(cb, pps * PAGE)
    nsteps = n_pages // pps
    kvw = N_KV_A * HEAD_DIM
    pk, pv, pik = _keys_on_lanes(pool_k), _keys_on_lanes(pool_v), _keys_on_lanes(pool_ik)

    def page_specs(block):
        zeros = (0,) * (len(block) - 2)
        return [pl.BlockSpec(block, lambda b, j, pt, r=r: (layer, pt[b, j * pps + r]) + zeros) for r in range(pps)]

    hrows = N_IDX_HEADS * t
    qi_r = qi.transpose(0, 2, 1, 3).reshape(db, hrows, IDX_DIM).astype(BF16)
    wi_r = wi.transpose(0, 2, 1).reshape(db, hrows, 1)
    nkeys = past + cb
    bias = pl.pallas_call(
        functools.partial(_dsa_sample_select_kernel, pps=pps, t_dec=t, topk=topk, past=past, cb=cb),
        grid_spec=pltpu.PrefetchScalarGridSpec(
            num_scalar_prefetch=1,
            grid=(db, nsteps),
            in_specs=[pl.BlockSpec((1, hrows, IDX_DIM), lambda b, j, pt: (b, 0, 0)),
                      pl.BlockSpec((1, hrows, 1), lambda b, j, pt: (b, 0, 0)),
                      pl.BlockSpec((1, IDX_DIM, PAGE), lambda b, j, pt: (b, 0, 0))]
                     + page_specs((1, 1, IDX_DIM, PAGE)),
            out_specs=pl.BlockSpec((1, t, nkeys), lambda b, j, pt: (b, 0, 0)),
            scratch_shapes=[pltpu.VMEM((t, nkeys), I32)],
        ),
        out_shape=jax.ShapeDtypeStruct((db, t, nkeys), F32),
        compiler_params=_params(("parallel", "arbitrary")),
        name="dsa_sample_select",
    )(page_table, qi_r, wi_r, _new_keys_on_lanes(ki), *([pik] * pps))

    q5 = (qa * (HEAD_DIM ** -0.5 * LOG2E)).reshape(db, t, N_KV_A, GROUP_A, HEAD_DIM).transpose(0, 2, 3, 1, 4)
    q5 = q5.reshape(db, N_KV_A, GROUP_A * t, HEAD_DIM)
    eye = jnp.eye(N_KV_A, dtype=F32)
    q_pad = (q5[:, :, :, None, :] * eye[None, :, None, :, None]).reshape(db, N_KV_A, GROUP_A * t, kvw).astype(BF16)
    rows = GROUP_A * t
    kv_block = (1, 1, N_KV_A, HEAD_DIM, PAGE)
    out = pl.pallas_call(
        functools.partial(_dsa_sample_attend_kernel, pps=pps, t_dec=t),
        grid_spec=pltpu.PrefetchScalarGridSpec(
            num_scalar_prefetch=1,
            grid=(db, nsteps),
            in_specs=[pl.BlockSpec((1, N_KV_A, rows, kvw), lambda b, j, pt: (b, 0, 0, 0)),
                      pl.BlockSpec((1, t, pps * PAGE), lambda b, j, pt: (b, 0, j)),
                      pl.BlockSpec((1, t, PAGE), lambda b, j, pt: (b, 0, past // PAGE)),
                      pl.BlockSpec((1, kvw, PAGE), lambda b, j, pt: (b, 0, 0)),
                      pl.BlockSpec((1, kvw, PAGE), lambda b, j, pt: (b, 0, 0))]
                     + page_specs(kv_block) + page_specs(kv_block),
            out_specs=pl.BlockSpec((1, N_KV_A, rows, kvw), lambda b, j, pt: (b, 0, 0, 0)),
            scratch_shapes=[pltpu.VMEM((N_KV_A, rows, 1), F32), pltpu.VMEM((N_KV_A, rows, 1), F32),
                            pltpu.VMEM((N_KV_A, rows, kvw), F32)],
        ),
        out_shape=jax.ShapeDtypeStruct((db, N_KV_A, rows, kvw), F32),
        compiler_params=_params(("parallel", "arbitrary")),
        name="dsa_sample_attend",
    )(page_table, q_pad, bias, bias, _new_keys_on_lanes(ka.reshape(db, t, kvw)), _new_keys_on_lanes(va.reshape(db, t, kvw)),
      *([pk] * pps), *([pv] * pps))
    o = out.reshape(db, N_KV_A, GROUP_A, t, N_KV_A, HEAD_DIM)
    o = jnp.stack([o[:, c, :, :, c, :] for c in range(N_KV_A)], axis=1)
    return o.transpose(0, 3, 1, 2, 4).reshape(db, t, N_HEADS_A * HEAD_DIM)


def _mem_kernel(x_ref, wq_ref, mk_ref, mv_ref, wo_ref, g_ref, b_ref, o_ref, *, alpha):
    x32 = x_ref[...].reshape(x_ref.shape[-2:])
    x = x32.astype(BF16)
    mk = mk_ref[...].reshape(mk_ref.shape[-2:]).astype(BF16)
    mv = mv_ref[...].reshape(mv_ref.shape[-2:]).astype(BF16)
    q = jnp.dot(x, wq_ref[...], preferred_element_type=F32).astype(BF16)
    outs = []
    for h in range(N_MEM_HEADS):
        sl = slice(h * MEM_HEAD_DIM, (h + 1) * MEM_HEAD_DIM)
        lg = _dot_nt(q[:, sl], mk[:, sl]) * (MEM_HEAD_DIM ** -0.5)
        p = jnp.exp(lg - lg.max(axis=1, keepdims=True))
        p = p / p.sum(axis=1, keepdims=True)
        outs.append(jnp.dot(p.astype(BF16), mv[:, sl], preferred_element_type=F32))
    o = jnp.concatenate(outs, axis=1).astype(BF16)
    attn = jnp.dot(o, wo_ref[...], preferred_element_type=F32)
    o_ref[...] = _residual_norm(x32, attn, alpha, g_ref[...], b_ref[...]).reshape(o_ref.shape)


def mem_attend(x, mk, mv, wq, wo, ln_g, ln_b, alpha, tm):
    b, t, d = x.shape
    n_mem, w = mk.shape[1:]
    row = pl.BlockSpec((1, d), lambda bi, i: (0, 0))
    return pl.pallas_call(
        functools.partial(_mem_kernel, alpha=alpha),
        grid=(b, t // tm),
        in_specs=[pl.BlockSpec((1, tm, d), lambda bi, i: (bi, i, 0)),
                  pl.BlockSpec(wq.shape, lambda bi, i: (0, 0)),
                  pl.BlockSpec((1, n_mem, w), lambda bi, i: (bi, 0, 0)),
                  pl.BlockSpec((1, n_mem, w), lambda bi, i: (bi, 0, 0)),
                  pl.BlockSpec(wo.shape, lambda bi, i: (0, 0)), row, row],
        out_specs=pl.BlockSpec((1, tm, d), lambda bi, i: (bi, i, 0)),
        out_shape=jax.ShapeDtypeStruct((b, t, d), F32),
        compiler_params=_params(("parallel", "parallel")),
        name="mem_attend",
    )(x, wq, mk, mv, wo, ln_g.reshape(1, d), ln_b.reshape(1, d))


def _split3_nt(w, x):
    wh = w.astype(BF16)
    wl = (w - wh.astype(F32)).astype(BF16)
    xh = x.astype(BF16)
    xl = (x - xh.astype(F32)).astype(BF16)
    return _dot_nt(wh, xh) + (_dot_nt(wh, xl) + _dot_nt(wl, xh))


def _router_kernel(x_ref, w_ref, b_ref, o_ref):
    logits = _split3_nt(w_ref[...], x_ref[...]) + b_ref[...]
    rows = [logits[e:e + 1, :] for e in range(N_EXPERTS)]
    mx = functools.reduce(jnp.maximum, rows)
    ex = [jnp.exp(r - mx) for r in rows]
    tot = functools.reduce(lambda a, b: a + b, ex)
    p = [e / tot for e in ex]
    gscore = []
    for g in range(N_GROUPS):
        a, b, c, d = p[4 * g:4 * g + 4]
        h1, l1, h2, l2 = jnp.maximum(a, b), jnp.minimum(a, b), jnp.maximum(c, d), jnp.minimum(c, d)
        gscore.append(jnp.maximum(h1, h2) + jnp.maximum(jnp.minimum(h1, h2), jnp.maximum(l1, l2)))
    best = gscore[0]
    group = jnp.zeros_like(best, dtype=I32)
    for g in range(1, N_GROUPS):
        better = gscore[g] > best
        best = jnp.where(better, gscore[g], best)
        group = jnp.where(better, g, group)
    sel = []
    for e in range(N_EXPERTS):
        g = e // EXPERTS_PER_GROUP
        rank = jnp.zeros_like(group)
        for o in range(g * EXPERTS_PER_GROUP, (g + 1) * EXPERTS_PER_GROUP):
            if o != e:
                ahead = (p[o] >= p[e]) if o < e else (p[o] > p[e])
                rank = rank + jnp.where(ahead, 1, 0)
        sel.append((group == g) & (rank < 2))
    top_sum = functools.reduce(lambda a, b: a + b, [jnp.where(s, pe, 0.0) for s, pe in zip(sel, p)])
    o_ref[...] = jnp.concatenate([jnp.where(s, pe / top_sum, 0.0) for s, pe in zip(sel, p)], axis=0)


def router(x, w_router_t, b_router, tm):
    n, d = x.shape
    return pl.pallas_call(
        _router_kernel,
        grid=(n // tm,),
        in_specs=[pl.BlockSpec((tm, d), lambda i: (i, 0)),
                  pl.BlockSpec((N_EXPERTS, d), lambda i: (0, 0)),
                  pl.BlockSpec((N_EXPERTS, 1), lambda i: (0, 0))],
        out_specs=pl.BlockSpec((N_EXPERTS, tm), lambda i: (0, i)),
        out_shape=jax.ShapeDtypeStruct((N_EXPERTS, n), F32),
        compiler_params=_params(("parallel",)),
        name="router",
    )(x, w_router_t, b_router.reshape(N_EXPERTS, 1))


def _moe_kernel(x_ref, wg_ref, wu_ref, wd_ref, comb_ref, g_ref, b_ref, o_ref, *, alpha):
    e = pl.program_id(1)
    x = x_ref[...].astype(BF16)
    g = jnp.dot(x, wg_ref[0], preferred_element_type=F32)
    u = jnp.dot(x, wu_ref[0], preferred_element_type=F32)
    h = (g * jax.nn.sigmoid(g)) * u
    y = jnp.dot(h.astype(BF16), wd_ref[0], preferred_element_type=F32)
    lane = lax.broadcasted_iota(I32, (1, N_EXPERTS), 1)
    ce = jnp.sum(jnp.where(lane == e, comb_ref[...], 0.0), axis=1, keepdims=True)

    @pl.when(e == 0)
    def _():
        o_ref[...] = ce * y

    @pl.when(e > 0)
    def _():
        o_ref[...] += ce * y

    @pl.when(e == N_EXPERTS - 1)
    def _():
        o_ref[...] = _residual_norm(x_ref[...], o_ref[...], alpha, g_ref[...], b_ref[...])


def moe_experts(x, comb, wg, wu, wd, ln_g, ln_b, alpha, tm):
    n, d = x.shape
    de = wg.shape[2]
    row = pl.BlockSpec((1, d), lambda i, e: (0, 0))
    return pl.pallas_call(
        functools.partial(_moe_kernel, alpha=alpha),
        grid=(n // tm, N_EXPERTS),
        in_specs=[pl.BlockSpec((tm, d), lambda i, e: (i, 0)),
                  pl.BlockSpec((1, d, de), lambda i, e: (e, 0, 0)),
                  pl.BlockSpec((1, d, de), lambda i, e: (e, 0, 0)),
                  pl.BlockSpec((1, de, d), lambda i, e: (e, 0, 0)),
                  pl.BlockSpec((tm, N_EXPERTS), lambda i, e: (i, 0)), row, row],
        out_specs=pl.BlockSpec((tm, d), lambda i, e: (i, 0)),
        out_shape=jax.ShapeDtypeStruct((n, d), F32),
        compiler_params=_params(("parallel", "arbitrary")),
        name="moe_experts",
    )(x, wg, wu, wd, comb, ln_g.reshape(1, d), ln_b.reshape(1, d))


def _rope(x, pos):
    rot = x.shape[-1] // 4
    half = rot // 2
    inv_freq = jnp.power(ROPE_THETA, -jnp.arange(half, dtype=F32) * 2.0 / rot)
    ang = pos.astype(F32)[:, None] * inv_freq[None, :]
    cos = jnp.cos(ang)[:, None, :]
    sin = jnp.sin(ang)[:, None, :]
    x1, x2 = x[..., :half], x[..., half:rot]
    return jnp.concatenate([x1 * cos - x2 * sin, x2 * cos + x1 * sin, x[..., rot:]], axis=-1)


def _pack_w_in(w_in_l):
    cols = [w_in_l[:, _SRC[n][0]:_SRC[n][0] + _SRC[n][1]] for n in _ORDER]
    packed = jnp.concatenate(cols, axis=1)
    return jnp.pad(packed, ((0, 0), (0, W_IN_PACKED - packed.shape[1]))).astype(BF16)


def _take(proj, name):
    o, n = _DST[name]
    return proj[..., o:o + n]


def _mixer_inputs(proj, pos):
    lead = proj.shape[:-1]
    hd = lambda name, h: _take(proj, name).reshape(lead + (h, HEAD_DIM))
    rp = lambda a: _rope(a, pos)
    qa, ka, va = rp(hd("qa", N_HEADS_A)), rp(hd("ka", N_KV_A)), hd("va", N_KV_A)
    qi = rp(hd("qi", N_IDX_HEADS))
    ki = rp(_take(proj, "ki")[..., None, :])[..., 0, :]
    wi = _take(proj, "wi")
    return qa, ka, va, qi, ki, wi, hd("kb", N_HEADS_B), hd("vb", N_HEADS_B)


def _tail(x1, mem_fn, lw, tm):
    x2 = mem_fn(x1)
    comb = router(x2, lw["w_router_t"], lw["b_router"], tm).T
    return moe_experts(x2, comb, lw["w_gate"], lw["w_up"], lw["w_down"], lw["ln3_g"], lw["ln3_b"], lw["alpha"], tm)


def kernel(x_prompt, x_sample, cache_a_k, cache_a_v, cache_idx_k, cache_b_k, cache_b_v, cache_mem_k, cache_mem_v,
           page_table, mem_prompt, w_in, w_pa, w_pb, w_o, ln1_g, ln1_b, w_cq, w_ck, w_cv, w_co, ln2_g, ln2_b,
           w_router, b_router, w_gate, w_up, w_down, ln3_g, ln3_b):
    depth = w_in.shape[0]
    bp, seq, d = x_prompt.shape
    db, t_dec, _ = x_sample.shape
    n_mem = mem_prompt.shape[1]
    past = page_table.shape[1] * PAGE
    k_prompt = min(MAX_SELECT, seq // 4)
    k_sample = min(MAX_SELECT, (past + t_dec) // 4)
    pos_p = jnp.arange(seq)
    pos_s = past + jnp.arange(t_dec)
    alpha = (2 * depth) ** 0.25
    tm_p = min(1024, seq)
    n_s = db * t_dec
    wm = N_MEM_HEADS * MEM_HEAD_DIM

    xp = x_prompt.reshape(bp * seq, d)
    xs = x_sample.reshape(n_s, d)
    outs = {k: [] for k in ("p_ak", "p_av", "p_ik", "p_bk", "p_bv", "p_mk", "p_mv", "s_ak", "s_av", "s_ik", "s_bk", "s_bv")}
    mem_bf = mem_prompt.reshape(bp * n_mem, d).astype(BF16)
    for l in range(depth):
        lw = dict(alpha=alpha, ln3_g=ln3_g[l], ln3_b=ln3_b[l], w_router_t=w_router.T, b_router=b_router,
                  w_gate=w_gate[l].astype(BF16), w_up=w_up[l].astype(BF16), w_down=w_down[l].astype(BF16))
        w_in_p = _pack_w_in(w_in[l])
        wpa, wpb, wo = w_pa[l].astype(BF16), w_pb[l].astype(BF16), w_o[l].astype(BF16)
        wcq, wco = w_cq[l].astype(BF16), w_co[l].astype(BF16)
        wckv = jnp.concatenate([w_ck[l], w_cv[l]], axis=1).astype(BF16)

        proj = matmul(xp, w_in_p, tm_p, 1024)
        proj_b = proj.reshape(bp, seq, W_IN_PACKED)
        qa, ka, va, qi, ki, wi, kb, vb = _mixer_inputs(proj_b, pos_p)
        ya = jnp.concatenate([dsa_prompt(qa[b], ka[b], va[b], qi[b], ki[b], wi[b], k_prompt) for b in range(bp)], axis=0)
        qb_s = (_take(proj_b, "qb") * (HEAD_DIM ** -0.5)).astype(BF16)
        kb_s = _take(proj_b, "kb").astype(BF16)
        vb_s = _take(proj_b, "vb").astype(BF16)
        yb = jnp.concatenate([sb_prompt(qb_s[b], kb_s[b], vb_s[b]) for b in range(bp)], axis=0)
        x1 = gated_merge(xp, ya, yb, proj, wpa, wpb, wo, ln1_g[l], ln1_b[l], alpha, tm_p)
        mkv = matmul(mem_bf, wckv, bp * n_mem, wm)
        mk, mv = mkv[:, :wm].reshape(bp, n_mem, wm), mkv[:, wm:].reshape(bp, n_mem, wm)
        mem_fn = lambda x: mem_attend(x.reshape(bp, seq, d), mk, mv, wcq, wco, ln2_g[l], ln2_b[l], alpha, tm_p).reshape(bp * seq, d)
        xp = _tail(x1, mem_fn, lw, tm_p)
        outs["p_ak"].append(ka); outs["p_av"].append(va); outs["p_ik"].append(ki)
        outs["p_bk"].append(kb); outs["p_bv"].append(vb)
        outs["p_mk"].append(mk.reshape(bp, n_mem, N_MEM_HEADS, MEM_HEAD_DIM))
        outs["p_mv"].append(mv.reshape(bp, n_mem, N_MEM_HEADS, MEM_HEAD_DIM))

        proj = matmul(xs, w_in_p, n_s, 1024)
        proj_b = proj.reshape(db, t_dec, W_IN_PACKED)
        qa, ka, va, qi, ki, wi, kb, vb = _mixer_inputs(proj_b, pos_s)
        ya = dsa_sample(qa, ka, va, qi, ki, wi, cache_a_k, cache_a_v, cache_idx_k, l, page_table, k_sample)
        yb = sb_sample(_take(proj_b, "qb"), _take(proj_b, "kb"), _take(proj_b, "vb"), cache_b_k, cache_b_v, l, page_table)
        x1 = gated_merge(xs, ya.reshape(n_s, -1), yb.reshape(n_s, -1), proj, wpa, wpb, wo, ln1_g[l], ln1_b[l], alpha, n_s)
        cmk = cache_mem_k[l].reshape(db, n_mem, wm)
        cmv = cache_mem_v[l].reshape(db, n_mem, wm)
        mem_fn = lambda x: mem_attend(x.reshape(db, t_dec, d), cmk, cmv, wcq, wco, ln2_g[l], ln2_b[l], alpha, t_dec).reshape(n_s, d)
        xs = _tail(x1, mem_fn, lw, n_s)
        outs["s_ak"].append(ka); outs["s_av"].append(va); outs["s_ik"].append(ki)
        outs["s_bk"].append(kb); outs["s_bv"].append(vb)

    st = lambda k: jnp.stack(outs[k])
    return (xp.reshape(bp, seq, d), xs.reshape(db, t_dec, d),
            st("p_ak"), st("p_av"), st("p_ik"), st("p_bk"), st("p_bv"), st("p_mk"), st("p_mv"),
            st("s_ak"), st("s_av"), st("s_ik"), st("s_bk"), st("s_bv"))
```
